```python
import math
import jax, jax.numpy as jnp
from jax import lax
import numpy as np

D_MODEL = 4096
BATCH = 2
SEQ = 4096
DEPTH = 2
DEC_BATCH = 16
DEC_SEQ = 64
PAST_LEN = 2048

CHUNK = 64
D_MIX = D_MODEL
D_ATT = D_MIX // 2
D_HEAD = 128
H_ATT = D_ATT // (2 * D_HEAD)
D_RWKV = D_MIX - D_ATT
N_RWKV = 64
H_RWKV = D_RWKV // N_RWKV
DECAY_LORA = 96
AAA_LORA = 96
GATE_LORA = 256
D_FF = 4 * D_MODEL
Q_BLOCK = 128
RMS_EPS = 1e-5
LN_X_EPS = 64e-5
ATT_COLS = 3 * D_ATT
RWKV_COLS = 3 * D_RWKV + DECAY_LORA + AAA_LORA + GATE_LORA
IN_COLS = ATT_COLS + RWKV_COLS
RWKV_SPLITS = (D_RWKV, 2 * D_RWKV, 3 * D_RWKV, 3 * D_RWKV + DECAY_LORA, 3 * D_RWKV + DECAY_LORA + AAA_LORA)

kernel_name = "hymba_diffattn_rwkv7_stream_step"


def rmsnorm(x, g):
    xf = x.astype(jnp.float32)
    y = xf * lax.rsqrt(jnp.mean(xf * xf, axis=-1, keepdims=True) + RMS_EPS)
    return (y * g.astype(jnp.float32)).astype(x.dtype)


def diff_attention(q, k, v, q_pos, k_pos, lam, lam_init, subln_w):
    B, Tq = q.shape[0], q.shape[1]
    Tk = k.shape[1]
    qf = q.astype(jnp.float32).reshape(B, Tq, H_ATT, 2, D_HEAD)
    kf = k.astype(jnp.float32).reshape(B, Tk, H_ATT, 2, D_HEAD)
    s = jnp.einsum("bqhmd,bkhmd->bhmqk", qf, kf) * (D_HEAD ** -0.5)
    allowed = (k_pos[None, :] // CHUNK) <= (q_pos[:, None] // CHUNK)
    s = jnp.where(allowed, s, -jnp.inf)
    p = jax.nn.softmax(s, axis=-1)
    a = p[:, :, 0] - lam * p[:, :, 1]
    o = jnp.einsum("bhqk,bkhe->bqhe", a, v.astype(jnp.float32))
    o = o * lax.rsqrt(jnp.mean(o * o, axis=-1, keepdims=True) + RMS_EPS)
    o = o * subln_w.astype(jnp.float32) * (1.0 - lam_init)
    return o.astype(q.dtype)


def prompt_diff_attention(q, k, v, lam, lam_init, subln_w):
    B, T = q.shape[0], q.shape[1]
    n_blk = T // Q_BLOCK
    qb = q.reshape(B, n_blk, Q_BLOCK, H_ATT, 2 * D_HEAD).transpose(1, 0, 2, 3, 4)
    pos = jnp.arange(T, dtype=jnp.int32)
    pb = pos.reshape(n_blk, Q_BLOCK)

    def one_block(args):
        qi, pi = args
        return diff_attention(qi, k, v, pi, pos, lam, lam_init, subln_w)

    o = lax.map(one_block, (qb, pb))
    return o.transpose(1, 0, 2, 3, 4).reshape(B, T, H_ATT, 2 * D_HEAD)


def rwkv7_time_mix(u, shift_prev, wkv0, mu, w0, w_up, a0, a_up, g_up, k_k, k_a, r_k, lnx_w, lnx_b):
    B, T, _ = u.shape
    u_prev = jnp.concatenate([shift_prev.astype(u.dtype), u[:, :-1]], axis=1)
    us = (u + (u_prev - u) * mu).astype(jnp.float32)
    r, k, v, wd, ad, gd = jnp.split(us, RWKV_SPLITS, axis=-1)
    w = w0 + jnp.einsum("btr,rc->btc", jnp.tanh(wd), w_up)
    w = -jax.nn.softplus(-w) - 0.5
    decay = jnp.exp(-jnp.exp(w))
    a = jax.nn.sigmoid(a0 + jnp.einsum("btr,rc->btc", ad, a_up))
    g = jnp.einsum("btr,rc->btc", jax.nn.sigmoid(gd), g_up)
    kk = (k * k_k).reshape(B, T, H_RWKV, N_RWKV)
    kk = kk / jnp.maximum(jnp.sqrt(jnp.sum(kk * kk, axis=-1, keepdims=True)), 1e-12)
    k = k * (1.0 + (a - 1.0) * k_a)
    rh = r.reshape(B, T, H_RWKV, N_RWKV)
    kh = k.reshape(B, T, H_RWKV, N_RWKV)
    vh = v.reshape(B, T, H_RWKV, N_RWKV)
    wh = decay.reshape(B, T, H_RWKV, N_RWKV)
    ah = a.reshape(B, T, H_RWKV, N_RWKV)

    def step(S, inp):
        r_t, w_t, k_t, v_t, kk_t, a_t = inp
        sa = jnp.einsum("bhvk,bhk->bhv", S, kk_t)
        S = (S * w_t[:, :, None, :]
             - sa[..., None] * (kk_t * a_t)[:, :, None, :]
             + v_t[..., None] * k_t[:, :, None, :])
        o_t = jnp.einsum("bhvk,bhk->bhv", S, r_t)
        return S, o_t

    xs = tuple(jnp.moveaxis(t, 1, 0) for t in (rh, wh, kh, vh, kk, ah))
    S_fin, o = lax.scan(step, wkv0.astype(jnp.float32), xs)
    o = jnp.moveaxis(o, 0, 1)
    mean = jnp.mean(o, axis=-1, keepdims=True)
    var = jnp.mean(jnp.square(o - mean), axis=-1, keepdims=True)
    o = ((o - mean) * lax.rsqrt(var + LN_X_EPS)).reshape(B, T, D_RWKV) * lnx_w + lnx_b
    bonus = jnp.sum(rh * kh * r_k, axis=-1, keepdims=True) * vh
    o = (o + bonus.reshape(B, T, D_RWKV)) * g
    return o.astype(u.dtype), u[:, -1:], S_fin.astype(wkv0.dtype)


def trunk_layer(x, l, p, past_k, past_v, shift_prev, wkv0):
    B, T, _ = x.shape
    xn = rmsnorm(x, p["norm_mix"][l])
    proj = jnp.einsum("btd,dc->btc", xn, p["w_in"][l])
    q = proj[..., :D_ATT].reshape(B, T, H_ATT, 2 * D_HEAD)
    k = proj[..., D_ATT:2 * D_ATT].reshape(B, T, H_ATT, 2 * D_HEAD)
    v = proj[..., 2 * D_ATT:ATT_COLS].reshape(B, T, H_ATT, 2 * D_HEAD)
    u = proj[..., ATT_COLS:]
    lam_init = 0.8 - 0.6 * math.exp(-0.3 * l)
    lam = (jnp.exp(jnp.sum(p["lam_q1"][l].astype(jnp.float32) * p["lam_k1"][l].astype(jnp.float32)))
           - jnp.exp(jnp.sum(p["lam_q2"][l].astype(jnp.float32) * p["lam_k2"][l].astype(jnp.float32)))
           + lam_init)
    if past_k is None:
        att = prompt_diff_attention(q, k, v, lam, lam_init, p["attn_subln"][l])
    else:
        Tp = past_k.shape[1]
        k_all = jnp.concatenate([past_k.astype(k.dtype), k], axis=1)
        v_all = jnp.concatenate([past_v.astype(v.dtype), v], axis=1)
        q_pos = Tp + jnp.arange(T, dtype=jnp.int32)
        k_pos = jnp.arange(Tp + T, dtype=jnp.int32)
        att = diff_attention(q, k_all, v_all, q_pos, k_pos, lam, lam_init, p["attn_subln"][l])
    rw, shift_new, wkv_new = rwkv7_time_mix(
        u, shift_prev, wkv0, p["shift_mu"][l], p["decay_w0"][l], p["decay_up"][l],
        p["iclr_a0"][l], p["iclr_up"][l], p["gate_up"][l], p["key_kk"][l], p["key_ka"][l],
        p["bonus_rk"][l], p["lnx_w"][l], p["lnx_b"][l])
    mix = jnp.concatenate([att.reshape(B, T, D_ATT), rw], axis=-1)
    x = x + jnp.einsum("btc,cd->btd", mix, p["w_out"][l])
    hn = rmsnorm(x, p["norm_ffn"][l])
    h = jnp.square(jax.nn.relu(jnp.einsum("btd,df->btf", hn, p["ffn_up"][l])))
    x = x + jnp.einsum("btf,fd->btd", h, p["ffn_down"][l])
    return x, k, v, shift_new, wkv_new


def setup_inputs(seed: int = 0) -> dict:
    key = jax.random.key(seed)
    ks = jax.random.split(key, 32)
    f32 = jnp.float32

    def nrm(k, shape, scale):
        return jax.random.normal(k, shape, f32) * scale

    return {
        "x_prompt": nrm(ks[0], (BATCH, SEQ, D_MODEL), 1.0),
        "x_sample": nrm(ks[1], (DEC_BATCH, DEC_SEQ, D_MODEL), 1.0),
        "cache_k": nrm(ks[2], (DEPTH, DEC_BATCH, PAST_LEN, H_ATT, 2 * D_HEAD), 1.0),
        "cache_v": nrm(ks[3], (DEPTH, DEC_BATCH, PAST_LEN, H_ATT, 2 * D_HEAD), 1.0),
        "state_shift": nrm(ks[4], (DEPTH, DEC_BATCH, 1, RWKV_COLS), 1.0),
        "state_wkv": nrm(ks[5], (DEPTH, DEC_BATCH, H_RWKV, N_RWKV, N_RWKV), 0.3),
        "norm_mix": 1.0 + nrm(ks[6], (DEPTH, D_MODEL), 0.02),
        "w_in": nrm(ks[7], (DEPTH, D_MODEL, IN_COLS), D_MODEL ** -0.5),
        "lam_q1": nrm(ks[8], (DEPTH, D_HEAD), 0.1),
        "lam_k1": nrm(ks[9], (DEPTH, D_HEAD), 0.1),
        "lam_q2": nrm(ks[10], (DEPTH, D_HEAD), 0.1),
        "lam_k2": nrm(ks[11], (DEPTH, D_HEAD), 0.1),
        "attn_subln": 1.0 + nrm(ks[12], (DEPTH, 2 * D_HEAD), 0.02),
        "shift_mu": jax.random.uniform(ks[13], (DEPTH, RWKV_COLS), f32),
        "decay_w0": nrm(ks[14], (DEPTH, D_RWKV), 1.0) - 1.5,
        "decay_up": nrm(ks[15], (DEPTH, DECAY_LORA, D_RWKV), 0.5 * DECAY_LORA ** -0.5),
        "iclr_a0": nrm(ks[16], (DEPTH, D_RWKV), 0.1),
        "iclr_up": nrm(ks[17], (DEPTH, AAA_LORA, D_RWKV), AAA_LORA ** -0.5),
        "gate_up": nrm(ks[18], (DEPTH, GATE_LORA, D_RWKV), GATE_LORA ** -0.5),
        "key_kk": 0.85 + nrm(ks[19], (DEPTH, D_RWKV), 0.02),
        "key_ka": 1.0 + nrm(ks[20], (DEPTH, D_RWKV), 0.02),
        "bonus_rk": nrm(ks[21], (DEPTH, H_RWKV, N_RWKV), 0.1),
        "lnx_w": 1.0 + nrm(ks[22], (DEPTH, D_RWKV), 0.02),
        "lnx_b": nrm(ks[23], (DEPTH, D_RWKV), 0.02),
        "w_out": nrm(ks[24], (DEPTH, D_MIX, D_MODEL), D_MIX ** -0.5),
        "norm_ffn": 1.0 + nrm(ks[25], (DEPTH, D_MODEL), 0.02),
        "ffn_up": nrm(ks[26], (DEPTH, D_MODEL, D_FF), D_MODEL ** -0.5),
        "ffn_down": nrm(ks[27], (DEPTH, D_FF, D_MODEL), D_FF ** -0.5),
        "norm_final": 1.0 + nrm(ks[28], (D_MODEL,), 0.02),
    }


def reference(x_prompt, x_sample, cache_k, cache_v, state_shift, state_wkv, norm_mix, w_in,
              lam_q1, lam_k1, lam_q2, lam_k2, attn_subln, shift_mu, decay_w0, decay_up, iclr_a0,
              iclr_up, gate_up, key_kk, key_ka, bonus_rk, lnx_w, lnx_b, w_out, norm_ffn, ffn_up,
              ffn_down, norm_final):
    p = dict(norm_mix=norm_mix, w_in=w_in, lam_q1=lam_q1, lam_k1=lam_k1, lam_q2=lam_q2,
             lam_k2=lam_k2, attn_subln=attn_subln, shift_mu=shift_mu, decay_w0=decay_w0,
             decay_up=decay_up, iclr_a0=iclr_a0, iclr_up=iclr_up, gate_up=gate_up,
             key_kk=key_kk, key_ka=key_ka, bonus_rk=bonus_rk, lnx_w=lnx_w, lnx_b=lnx_b,
             w_out=w_out, norm_ffn=norm_ffn, ffn_up=ffn_up, ffn_down=ffn_down)
    B = x_prompt.shape[0]
    shift_zero = jnp.zeros((B, 1, RWKV_COLS), x_prompt.dtype)
    wkv_zero = jnp.zeros((B, H_RWKV, N_RWKV, N_RWKV), state_wkv.dtype)
    xp, xs = x_prompt, x_sample
    kp_l, vp_l, sp_l, wp_l = [], [], [], []
    ks_l, vs_l, ss_l, ws_l = [], [], [], []
    for l in range(DEPTH):
        xp, kp, vp, sp, wp = trunk_layer(xp, l, p, None, None, shift_zero, wkv_zero)
        xs, k_s, v_s, s_s, w_s = trunk_layer(xs, l, p, cache_k[l], cache_v[l], state_shift[l], state_wkv[l])
        kp_l.append(kp); vp_l.append(vp); sp_l.append(sp); wp_l.append(wp)
        ks_l.append(k_s); vs_l.append(v_s); ss_l.append(s_s); ws_l.append(w_s)
    y_prompt = rmsnorm(xp, norm_final)
    y_sample = rmsnorm(xs, norm_final)
    return (y_prompt, y_sample,
            jnp.stack(kp_l), jnp.stack(vp_l), jnp.stack(sp_l), jnp.stack(wp_l),
            jnp.stack(ks_l), jnp.stack(vs_l), jnp.stack(ss_l), jnp.stack(ws_l))
```

```python
import functools
import math

import jax
import jax.numpy as jnp
from jax import lax
from jax.experimental import pallas as pl
from jax.experimental.pallas import tpu as pltpu

F32 = jnp.float32
BF16 = jnp.bfloat16

D_HEAD = 128
HEAD_W = 2 * D_HEAD
CHUNK = 64
N_RWKV = 64
DECAY_LORA = 96
AAA_LORA = 96
GATE_LORA = 256
RMS_EPS = 1e-5
LN_X_EPS = 64e-5
NEG = -1e30

LANES = 128
SUBLANES = 8
VMEM_LIMIT = 56 * 1024 * 1024

LORA_W_OFF, LORA_A_OFF, LORA_G_OFF, LORA_PAD = 0, 128, 256, 512
SEG = 64
PREP_ROWS = 128


def _pick(n, pref, mult):
    best = None
    for d in range(mult, min(n, pref) + 1, mult):
        if n % d == 0:
            best = d
    assert best is not None, (n, pref, mult)
    return best


def _params(sem):
    return pltpu.CompilerParams(dimension_semantics=sem, vmem_limit_bytes=VMEM_LIMIT)


def _rmsnorm_kernel(x_ref, g_ref, o_ref):
    x = x_ref[...]
    ms = jnp.mean(x * x, axis=-1, keepdims=True)
    o_ref[...] = (x * lax.rsqrt(ms + RMS_EPS) * g_ref[...]).astype(o_ref.dtype)


def _rmsnorm(x, g, out_dtype):
    n, d = x.shape
    tr = _pick(n, 256, SUBLANES)
    return pl.pallas_call(
        _rmsnorm_kernel,
        out_shape=jax.ShapeDtypeStruct((n, d), out_dtype),
        grid=(n // tr,),
        in_specs=[pl.BlockSpec((tr, d), lambda i: (i, 0)),
                  pl.BlockSpec((1, d), lambda i: (0, 0))],
        out_specs=pl.BlockSpec((tr, d), lambda i: (i, 0)),
        compiler_params=_params(("parallel",)),
        name="rmsnorm",
    )(x, g.reshape(1, d))


def _mm_kernel(*refs, nk, has_res, act):
    if has_res:
        x_ref, w_ref, r_ref, o_ref, acc_ref = refs
    else:
        x_ref, w_ref, o_ref, acc_ref = refs
    k = pl.program_id(2)

    @pl.when(k == 0)
    def _():
        acc_ref[...] = jnp.zeros_like(acc_ref)

    acc_ref[...] += jnp.dot(x_ref[...], w_ref[...], preferred_element_type=F32)

    @pl.when(k == nk - 1)
    def _():
        y = acc_ref[...]
        if act == "relu2":
            y = jnp.square(jnp.maximum(y, 0.0))
        if has_res:
            y = y + r_ref[...]
        o_ref[...] = y.astype(o_ref.dtype)


def _matmul(x, w, *, residual=None, act=None, out_dtype=F32, tm=1024, tn=1024, tk=1024, name="matmul"):
    m, kd = x.shape
    _, n = w.shape
    tm, tn, tk = _pick(m, tm, SUBLANES), _pick(n, tn, LANES), _pick(kd, tk, LANES)
    nk = kd // tk
    in_specs = [pl.BlockSpec((tm, tk), lambda i, j, k: (i, k)),
                pl.BlockSpec((tk, tn), lambda i, j, k: (k, j))]
    args = [x, w]
    if residual is not None:
        in_specs.append(pl.BlockSpec((tm, tn), lambda i, j, k: (i, j)))
        args.append(residual)
    return pl.pallas_call(
        functools.partial(_mm_kernel, nk=nk, has_res=residual is not None, act=act),
        out_shape=jax.ShapeDtypeStruct((m, n), out_dtype),
        grid=(m // tm, n // tn, nk),
        in_specs=in_specs,
        out_specs=pl.BlockSpec((tm, tn), lambda i, j, k: (i, j)),
        scratch_shapes=[pltpu.VMEM((tm, tn), F32)],
        compiler_params=_params(("parallel", "parallel", "arbitrary")),
        name=name,
    )(*args)


def _softmax_tile_update(qs_ref, kb, vb, mask, m_ref, l_ref, acc_ref):
    for br in range(2):
        qm = qs_ref[:, br * D_HEAD:(br + 1) * D_HEAD]
        km = kb[:, br * D_HEAD:(br + 1) * D_HEAD]
        s = lax.dot_general(qm, km, (((1,), (1,)), ((), ())), preferred_element_type=F32)
        if mask is not None:
            s = jnp.where(mask, s, NEG)
        m_prev = m_ref[br]
        m_new = jnp.maximum(m_prev, jnp.max(s, axis=-1, keepdims=True))
        alpha = jnp.exp(m_prev - m_new)
        p = jnp.exp(s - m_new)
        l_ref[br] = alpha * l_ref[br] + jnp.sum(p, axis=-1, keepdims=True)
        acc_ref[br] = alpha * acc_ref[br] + jnp.dot(p.astype(BF16), vb, preferred_element_type=F32)
        m_ref[br] = m_new


def _attn_init(q_ref, qs_ref, m_ref, l_ref, acc_ref):
    qs_ref[...] = (q_ref[...] * (D_HEAD ** -0.5)).astype(BF16)
    m_ref[...] = jnp.full(m_ref.shape, NEG, F32)
    l_ref[...] = jnp.zeros(l_ref.shape, F32)
    acc_ref[...] = jnp.zeros(acc_ref.shape, F32)


def _attn_finish(lam_ref, subln_ref, o_ref, l_ref, acc_ref, lam_init):
    lp = lam_ref[...]
    lam = (jnp.exp(jnp.sum(lp[0:1] * lp[1:2], axis=-1, keepdims=True))
           - jnp.exp(jnp.sum(lp[2:3] * lp[3:4], axis=-1, keepdims=True)) + lam_init)
    o = acc_ref[0] / l_ref[0] - lam * (acc_ref[1] / l_ref[1])
    o = o * lax.rsqrt(jnp.mean(o * o, axis=-1, keepdims=True) + RMS_EPS)
    o_ref[...] = (o * subln_ref[...] * (1.0 - lam_init)).astype(o_ref.dtype)


def _attn_prompt_kernel(lam_ref, subln_ref, q_ref, k_ref, v_ref, o_ref, qs_ref, m_ref, l_ref, acc_ref,
                        *, tq, lam_init):
    i = pl.program_id(2)
    _attn_init(q_ref, qs_ref, m_ref, l_ref, acc_ref)

    def tile(j, mask):
        kb = k_ref[pl.ds(pl.multiple_of(j * tq, tq), tq), :].astype(BF16)
        vb = v_ref[pl.ds(pl.multiple_of(j * tq, tq), tq), :].astype(BF16)
        _softmax_tile_update(qs_ref, kb, vb, mask, m_ref, l_ref, acc_ref)

    def body(j, carry):
        tile(j, None)
        return carry

    lax.fori_loop(0, i, body, 0)
    row_chunk = lax.broadcasted_iota(jnp.int32, (tq, tq), 0) // CHUNK
    col_chunk = lax.broadcasted_iota(jnp.int32, (tq, tq), 1) // CHUNK
    tile(i, col_chunk <= row_chunk)
    _attn_finish(lam_ref, subln_ref, o_ref, l_ref, acc_ref, lam_init)


def _attn_prompt(proj, lam_p, subln, *, batch, seq, h_att, lam_init):
    tq = _pick(seq, 256, CHUNK)
    nq = seq // tq
    return pl.pallas_call(
        functools.partial(_attn_prompt_kernel, tq=tq, lam_init=lam_init),
        out_shape=jax.ShapeDtypeStruct((batch * seq, h_att * HEAD_W), BF16),
        grid=(batch, h_att, nq),
        in_specs=[pl.BlockSpec((4, D_HEAD), lambda b, h, i: (0, 0)),
                  pl.BlockSpec((1, HEAD_W), lambda b, h, i: (0, 0)),
                  pl.BlockSpec((tq, HEAD_W), lambda b, h, i: (b * nq + i, h)),
                  pl.BlockSpec((seq, HEAD_W), lambda b, h, i: (b, h_att + h)),
                  pl.BlockSpec((seq, HEAD_W), lambda b, h, i: (b, 2 * h_att + h))],
        out_specs=pl.BlockSpec((tq, HEAD_W), lambda b, h, i: (b * nq + i, h)),
        scratch_shapes=[pltpu.VMEM((tq, HEAD_W), BF16),
                        pltpu.VMEM((2, tq, 1), F32),
                        pltpu.VMEM((2, tq, 1), F32),
                        pltpu.VMEM((2, tq, HEAD_W), F32)],
        compiler_params=_params(("parallel", "parallel", "arbitrary")),
        name="attn_prompt",
    )(lam_p, subln, proj, proj, proj)


def _attn_sample_kernel(lam_ref, subln_ref, q_ref, kn_ref, vn_ref, kp_ref, vp_ref, o_ref,
                        qs_ref, m_ref, l_ref, acc_ref, *, lam_init):
    _attn_init(q_ref, qs_ref, m_ref, l_ref, acc_ref)
    _softmax_tile_update(qs_ref, kp_ref[...].astype(BF16), vp_ref[...].astype(BF16), None, m_ref, l_ref, acc_ref)
    _softmax_tile_update(qs_ref, kn_ref[...].astype(BF16), vn_ref[...].astype(BF16), None, m_ref, l_ref, acc_ref)
    _attn_finish(lam_ref, subln_ref, o_ref, l_ref, acc_ref, lam_init)


def _attn_sample(proj, cache_k, cache_v, layer, lam_p, subln, *, row0, batch, seq, h_att, lam_init):
    past = cache_k.shape[2]
    assert past % CHUNK == 0 and seq <= CHUNK and row0 % seq == 0
    rb0 = row0 // seq
    return pl.pallas_call(
        functools.partial(_attn_sample_kernel, lam_init=lam_init),
        out_shape=jax.ShapeDtypeStruct((batch * seq, h_att * HEAD_W), BF16),
        grid=(batch, h_att),
        in_specs=[pl.BlockSpec((4, D_HEAD), lambda b, h: (0, 0)),
                  pl.BlockSpec((1, HEAD_W), lambda b, h: (0, 0)),
                  pl.BlockSpec((seq, HEAD_W), lambda b, h: (rb0 + b, h)),
                  pl.BlockSpec((seq, HEAD_W), lambda b, h: (rb0 + b, h_att + h)),
                  pl.BlockSpec((seq, HEAD_W), lambda b, h: (rb0 + b, 2 * h_att + h)),
                  pl.BlockSpec((None, None, past, HEAD_W), lambda b, h: (layer, b, 0, h)),
                  pl.BlockSpec((None, None, past, HEAD_W), lambda b, h: (layer, b, 0, h))],
        out_specs=pl.BlockSpec((seq, HEAD_W), lambda b, h: (b, h)),
        scratch_shapes=[pltpu.VMEM((seq, HEAD_W), BF16),
                        pltpu.VMEM((2, seq, 1), F32),
                        pltpu.VMEM((2, seq, 1), F32),
                        pltpu.VMEM((2, seq, HEAD_W), F32)],
        compiler_params=_params(("parallel", "parallel")),
        name="attn_sample",
    )(lam_p, subln, proj, proj, proj, cache_k, cache_v)


def _dot_f32(a, b):
    return jnp.dot(a, b, precision=lax.Precision.HIGHEST, preferred_element_type=F32)


def _rwkv_prep_kernel(ur_ref, uk_ref, uv_ref, ul_ref, halo_ref, mu_ref, w0_ref, wup_ref, a0_ref, aup_ref,
                      gup_ref, kkw_ref, kaw_ref, rkw_ref, gsum_ref, gexp_ref,
                      r_out, w_out, k_out, v_out, kk_out, b_out, bonus_out, g_out, *, tt, d_rwkv):
    nseg = tt // SEG
    first_row = lax.broadcasted_iota(jnp.int32, (SEG, 1), 0) == 0

    def shifted(u_ref, col0, width):
        parts = []
        for s in range(nseg):
            u = u_ref[s * SEG:(s + 1) * SEG, :]
            prev = jnp.where(first_row, halo_ref[s:s + 1, col0:col0 + width], pltpu.roll(u, 1, 0))
            parts.append(u + (prev - u) * mu_ref[:, col0:col0 + width])
        return jnp.concatenate(parts, axis=0) if nseg > 1 else parts[0]

    r = shifted(ur_ref, 0, d_rwkv)
    k = shifted(uk_ref, d_rwkv, d_rwkv)
    v = shifted(uv_ref, 2 * d_rwkv, d_rwkv)
    lo = shifted(ul_ref, 3 * d_rwkv, LORA_PAD)
    wd = lo[:, LORA_W_OFF:LORA_A_OFF]
    ad = lo[:, LORA_A_OFF:LORA_G_OFF]
    gd = lo[:, LORA_G_OFF:LORA_PAD]

    w = w0_ref[...] + _dot_f32(jnp.tanh(wd), wup_ref[...])
    x = -w
    w = -(jnp.maximum(x, 0.0) + jnp.log1p(jnp.exp(-jnp.abs(x)))) - 0.5
    decay = jnp.exp(-jnp.exp(w))
    a = jax.nn.sigmoid(a0_ref[...] + _dot_f32(ad, aup_ref[...]))
    g = _dot_f32(jax.nn.sigmoid(gd), gup_ref[...])

    kk = k * kkw_ref[...]
    ss = _dot_f32(kk * kk, gsum_ref[...])
    inv = 1.0 / jnp.maximum(jnp.sqrt(ss), 1e-12)
    kk = kk * _dot_f32(inv, gexp_ref[...])
    kmod = k * (1.0 + (a - 1.0) * kaw_ref[...])
    coef = _dot_f32(r * kmod * rkw_ref[...], gsum_ref[...])

    r_out[...] = r
    w_out[...] = decay
    k_out[...] = kmod
    v_out[...] = v
    kk_out[...] = kk
    b_out[...] = kk * a
    bonus_out[...] = _dot_f32(coef, gexp_ref[...]) * v
    g_out[...] = g


def _rwkv_prep(proj, halo, mu, w0, wup, a0, aup, gup, kkw, kaw, rkw, gsum, gexp, *, col0, d_rwkv):
    n = proj.shape[0]
    tt = PREP_ROWS
    assert n % tt == 0 and col0 % d_rwkv == 0 and (col0 + 3 * d_rwkv) % LORA_PAD == 0
    cb = col0 // d_rwkv
    lb = (col0 + 3 * d_rwkv) // LORA_PAD
    row = lambda width: pl.BlockSpec((1, width), lambda i: (0, 0))
    full = lambda a: pl.BlockSpec(a.shape, lambda i: (0, 0))
    tile = pl.BlockSpec((tt, d_rwkv), lambda i: (i, 0))
    return pl.pallas_call(
        functools.partial(_rwkv_prep_kernel, tt=tt, d_rwkv=d_rwkv),
        out_shape=[jax.ShapeDtypeStruct((n, d_rwkv), F32)] * 8,
        grid=(n // tt,),
        in_specs=[pl.BlockSpec((tt, d_rwkv), lambda i: (i, cb)),
                  pl.BlockSpec((tt, d_rwkv), lambda i: (i, cb + 1)),
                  pl.BlockSpec((tt, d_rwkv), lambda i: (i, cb + 2)),
                  pl.BlockSpec((tt, LORA_PAD), lambda i: (i, lb)),
                  pl.BlockSpec((None, tt // SEG, halo.shape[-1]), lambda i: (i, 0, 0)),
                  row(mu.shape[-1]), row(d_rwkv), full(wup), row(d_rwkv), full(aup), full(gup),
                  row(d_rwkv), row(d_rwkv), row(d_rwkv), full(gsum), full(gexp)],
        out_specs=[tile] * 8,
        compiler_params=_params(("parallel",)),
        name="rwkv_prep",
    )(proj, proj, proj, proj, halo, mu, w0, wup, a0, aup, gup, kkw, kaw, rkw, gsum, gexp)


def _rwkv_scan_kernel(r_ref, w_ref, k_ref, kk_ref, b_ref, v_ref, s0_ref, o_ref, sout_ref, s_ref, *, tc, nchunks):
    c = pl.program_id(1)
    nkey = s_ref.shape[0]

    @pl.when(c == 0)
    def _():
        s_ref[...] = s0_ref[...]

    def step(t, carry):
        vt = v_ref[t]
        sa = [None] * 4
        for k in range(nkey):
            term = s_ref[k] * kk_ref[t, k:k + 1, :]
            sa[k % 4] = term if sa[k % 4] is None else sa[k % 4] + term
        sa = (sa[0] + sa[1]) + (sa[2] + sa[3])
        oo = [None] * 4
        for k in range(nkey):
            s_new = s_ref[k] * w_ref[t, k:k + 1, :] - sa * b_ref[t, k:k + 1, :] + vt * k_ref[t, k:k + 1, :]
            s_ref[k] = s_new
            term = s_new * r_ref[t, k:k + 1, :]
            oo[k % 4] = term if oo[k % 4] is None else oo[k % 4] + term
        o_ref[t] = (oo[0] + oo[1]) + (oo[2] + oo[3])
        return carry

    lax.fori_loop(0, tc, step, 0)

    @pl.when(c == nchunks - 1)
    def _():
        sout_ref[...] = s_ref[...]


def _rwkv_scan(r, w, k, kk, b, v, s0):
    g, t, nkey, _ = r.shape
    vs = v.shape[2]
    tc = _pick(t, 32, 1)
    nchunks = t // tc
    op = pl.BlockSpec((None, tc, nkey, LANES), lambda gi, c: (gi, c, 0, 0))
    vp = pl.BlockSpec((None, tc, vs, LANES), lambda gi, c: (gi, c, 0, 0))
    sp = pl.BlockSpec((None, nkey, vs, LANES), lambda gi, c: (gi, 0, 0, 0))
    return pl.pallas_call(
        functools.partial(_rwkv_scan_kernel, tc=tc, nchunks=nchunks),
        out_shape=[jax.ShapeDtypeStruct((g, t, vs, LANES), F32),
                   jax.ShapeDtypeStruct((g, nkey, vs, LANES), F32)],
        grid=(g, nchunks),
        in_specs=[op, op, op, op, op, vp, sp],
        out_specs=[vp, sp],
        scratch_shapes=[pltpu.VMEM((nkey, vs, LANES), F32)],
        compiler_params=_params(("parallel", "arbitrary")),
        name="rwkv_scan",
    )(r, w, k, kk, b, v, s0)


def _rwkv_post_kernel(o_ref, bonus_ref, g_ref, lw_ref, lb_ref, gsum_ref, gexp_ref, out_ref):
    o = o_ref[...]
    inv_n = 1.0 / N_RWKV
    mean = _dot_f32(_dot_f32(o, gsum_ref[...]) * inv_n, gexp_ref[...])
    d = o - mean
    var = _dot_f32(d * d, gsum_ref[...]) * inv_n
    rstd = _dot_f32(lax.rsqrt(var + LN_X_EPS), gexp_ref[...])
    y = d * rstd * lw_ref[...] + lb_ref[...]
    out_ref[...] = ((y + bonus_ref[...]) * g_ref[...]).astype(out_ref.dtype)


def _rwkv_post(o, bonus, g, lnx_w, lnx_b, gsum, gexp):
    n, d = o.shape
    tt = _pick(n, 256, SUBLANES)
    tile = pl.BlockSpec((tt, d), lambda i: (i, 0))
    row = pl.BlockSpec((1, d), lambda i: (0, 0))
    full = lambda a: pl.BlockSpec(a.shape, lambda i: (0, 0))
    return pl.pallas_call(
        _rwkv_post_kernel,
        out_shape=jax.ShapeDtypeStruct((n, d), BF16),
        grid=(n // tt,),
        in_specs=[tile, tile, tile, row, row, full(gsum), full(gexp)],
        out_specs=tile,
        compiler_params=_params(("parallel",)),
        name="rwkv_post",
    )(o, bonus, g, lnx_w, lnx_b, gsum, gexp)


def _lane_plan(batch, h):
    pairs = batch * h
    if pairs >= LANES:
        assert pairs % LANES == 0
        return pairs // LANES, 1
    assert LANES % pairs == 0
    return 1, LANES // pairs


def _to_scan_rows(x, batch, t, h):
    groups, dup = _lane_plan(batch, h)
    y = x.reshape(batch, t, h, N_RWKV).transpose(1, 3, 0, 2).reshape(t, N_RWKV, batch * h)
    if dup > 1:
        return jnp.tile(y, (1, 1, dup))[None]
    return y.reshape(t, N_RWKV, groups, LANES).transpose(2, 0, 1, 3)


def _to_scan_v(x, batch, t, h):
    groups, dup = _lane_plan(batch, h)
    vs = N_RWKV // dup
    y = x.reshape(batch, t, h, dup, vs).transpose(1, 4, 3, 0, 2).reshape(t, vs, dup * batch * h)
    return y.reshape(t, vs, groups, LANES).transpose(2, 0, 1, 3)


def _from_scan_v(o, batch, t, h):
    groups, dup = _lane_plan(batch, h)
    vs = N_RWKV // dup
    y = o.transpose(1, 2, 0, 3).reshape(t, vs, dup, batch, h)
    return y.transpose(3, 0, 4, 2, 1).reshape(batch * t, h * N_RWKV)


def _state_to_scan(s, batch, h):
    groups, dup = _lane_plan(batch, h)
    vs = N_RWKV // dup
    y = s.reshape(batch, h, dup, vs, N_RWKV).transpose(4, 3, 2, 0, 1).reshape(N_RWKV, vs, dup * batch * h)
    return y.reshape(N_RWKV, vs, groups, LANES).transpose(2, 0, 1, 3)


def _state_from_scan(s, batch, h):
    groups, dup = _lane_plan(batch, h)
    vs = N_RWKV // dup
    y = s.transpose(1, 2, 0, 3).reshape(N_RWKV, vs, dup, batch, h)
    return y.transpose(3, 4, 2, 1, 0).reshape(batch, h, N_RWKV, N_RWKV)


def _pad_rwkv_cols(a, d_rwkv):
    z = jnp.zeros(a.shape[:-1] + (LORA_A_OFF - DECAY_LORA,), a.dtype)
    o = 3 * d_rwkv
    return jnp.concatenate([a[..., :o + DECAY_LORA], z,
                            a[..., o + DECAY_LORA:o + DECAY_LORA + AAA_LORA], z,
                            a[..., o + DECAY_LORA + AAA_LORA:]], axis=-1)


def _unpad_rwkv_cols(a, d_rwkv):
    o = 3 * d_rwkv
    return jnp.concatenate([a[..., :o + DECAY_LORA],
                            a[..., o + LORA_A_OFF:o + LORA_A_OFF + AAA_LORA],
                            a[..., o + LORA_G_OFF:]], axis=-1)


def _pad_rows(w, rows):
    return jnp.concatenate([w, jnp.zeros((rows - w.shape[0], w.shape[1]), w.dtype)], axis=0)


def kernel(x_prompt, x_sample, cache_k, cache_v, state_shift, state_wkv, norm_mix, w_in, lam_q1, lam_k1, lam_q2, lam_k2, attn_subln, shift_mu, decay_w0, decay_up, iclr_a0, iclr_up, gate_up, key_kk, key_ka, bonus_rk, lnx_w, lnx_b, w_out, norm_ffn, ffn_up, ffn_down, norm_final):
    bp, tp, d_model = x_prompt.shape
    bs, ts, _ = x_sample.shape
    depth = w_in.shape[0]
    d_att = d_model // 2
    h_att = d_att // HEAD_W
    d_rwkv = d_model - d_att
    h_rwkv = d_rwkv // N_RWKV
    att_cols = 3 * d_att
    n_p, n_s = bp * tp, bs * ts
    n = n_p + n_s
    past = cache_k.shape[2]
    assert tp % SEG == 0 and ts % SEG == 0 and SEG == CHUNK

    x = jnp.concatenate([x_prompt.reshape(n_p, d_model), x_sample.reshape(n_s, d_model)], axis=0)
    ck = cache_k.reshape(depth, bs, past, h_att * HEAD_W)
    cv = cache_v.reshape(depth, bs, past, h_att * HEAD_W)

    head_of = jnp.arange(d_rwkv, dtype=jnp.int32) // N_RWKV
    gsum = (head_of[:, None] == jnp.arange(LANES, dtype=jnp.int32)[None, :]).astype(F32)
    gexp = gsum.T

    seg_p = jnp.arange(n_p // SEG) % (tp // SEG) == 0
    seg_s = jnp.arange(n_s // SEG) % (ts // SEG) == 0

    outs = {name: [] for name in ("kp", "vp", "sp", "wp", "ks", "vs", "ss", "ws")}
    for l in range(depth):
        lam_init = 0.8 - 0.6 * math.exp(-0.3 * l)
        w_in_l = jnp.concatenate([w_in[l][:, :att_cols], _pad_rwkv_cols(w_in[l][:, att_cols:], d_rwkv)],
                                 axis=1).astype(BF16)
        xn = _rmsnorm(x, norm_mix[l], BF16)
        proj = _matmul(xn, w_in_l, tn=1280, name="w_in")

        lam_p = jnp.stack([lam_q1[l], lam_k1[l], lam_q2[l], lam_k2[l]])
        subln = attn_subln[l].reshape(1, HEAD_W)
        att_p = _attn_prompt(proj, lam_p, subln, batch=bp, seq=tp, h_att=h_att, lam_init=lam_init)
        att_s = _attn_sample(proj, ck, cv, l, lam_p, subln, row0=n_p, batch=bs, seq=ts, h_att=h_att,
                             lam_init=lam_init)

        u_last = proj[SEG - 1::SEG, att_cols:]
        prev = jnp.concatenate([jnp.zeros_like(u_last[:1]), u_last[:-1]], axis=0)
        st_p = jnp.zeros((n_p // SEG, u_last.shape[-1]), F32)
        st_s = jnp.repeat(_pad_rwkv_cols(state_shift[l][:, 0, :], d_rwkv), ts // SEG, axis=0)
        start = jnp.concatenate([seg_p, seg_s])[:, None]
        halo = jnp.where(start, jnp.concatenate([st_p, st_s], axis=0), prev)
        halo = halo.reshape(n // PREP_ROWS, PREP_ROWS // SEG, halo.shape[-1])

        mu = _pad_rwkv_cols(shift_mu[l], d_rwkv).reshape(1, -1)
        r, w, k, v, kk, b, bonus, g = _rwkv_prep(
            proj, halo, mu, decay_w0[l].reshape(1, -1), _pad_rows(decay_up[l], LORA_A_OFF - LORA_W_OFF),
            iclr_a0[l].reshape(1, -1), _pad_rows(iclr_up[l], LORA_G_OFF - LORA_A_OFF), gate_up[l],
            key_kk[l].reshape(1, -1), key_ka[l].reshape(1, -1), bonus_rk[l].reshape(1, -1), gsum, gexp,
            col0=att_cols, d_rwkv=d_rwkv)

        def scan(lo, hi, batch, t, s0):
            rows = [_to_scan_rows(a[lo:hi], batch, t, h_rwkv) for a in (r, w, k, kk, b)]
            o, s_fin = _rwkv_scan(*rows, _to_scan_v(v[lo:hi], batch, t, h_rwkv), _state_to_scan(s0, batch, h_rwkv))
            return _from_scan_v(o, batch, t, h_rwkv), _state_from_scan(s_fin, batch, h_rwkv)

        o_p, wkv_p = scan(0, n_p, bp, tp, jnp.zeros((bp, h_rwkv, N_RWKV, N_RWKV), F32))
        o_s, wkv_s = scan(n_p, n, bs, ts, state_wkv[l])
        rw = _rwkv_post(jnp.concatenate([o_p, o_s], axis=0), bonus, g, lnx_w[l].reshape(1, -1),
                        lnx_b[l].reshape(1, -1), gsum, gexp)

        mix = jnp.concatenate([jnp.concatenate([att_p, att_s], axis=0), rw], axis=1)
        x = _matmul(mix, w_out[l].astype(BF16), residual=x, name="w_out")
        hn = _rmsnorm(x, norm_ffn[l], BF16)
        hid = _matmul(hn, ffn_up[l].astype(BF16), act="relu2", out_dtype=BF16, name="ffn_up")
        x = _matmul(hid, ffn_down[l].astype(BF16), residual=x, name="ffn_down")

        k_new = proj[:, d_att:2 * d_att]
        v_new = proj[:, 2 * d_att:att_cols]
        u_fin_p = _unpad_rwkv_cols(proj[:n_p].reshape(bp, tp, -1)[:, -1:, att_cols:], d_rwkv)
        u_fin_s = _unpad_rwkv_cols(proj[n_p:].reshape(bs, ts, -1)[:, -1:, att_cols:], d_rwkv)
        outs["kp"].append(k_new[:n_p].reshape(bp, tp, h_att, HEAD_W))
        outs["vp"].append(v_new[:n_p].reshape(bp, tp, h_att, HEAD_W))
        outs["sp"].append(u_fin_p)
        outs["wp"].append(wkv_p)
        outs["ks"].append(k_new[n_p:].reshape(bs, ts, h_att, HEAD_W))
        outs["vs"].append(v_new[n_p:].reshape(bs, ts, h_att, HEAD_W))
        outs["ss"].append(u_fin_s)
        outs["ws"].append(wkv_s)

    y = _rmsnorm(x, norm_final, F32)
    return (y[:n_p].reshape(bp, tp, d_model), y[n_p:].reshape(bs, ts, d_model),
            jnp.stack(outs["kp"]), jnp.stack(outs["vp"]), jnp.stack(outs["sp"]), jnp.stack(outs["wp"]),
            jnp.stack(outs["ks"]), jnp.stack(outs["vs"]), jnp.stack(outs["ss"]), jnp.stack(outs["ws"]))
```

```python
import functools
import math

import jax
import jax.numpy as jnp
from jax import lax
from jax.experimental import pallas as pl
from jax.experimental.pallas import tpu as pltpu

F32 = jnp.float32
BF16 = jnp.bfloat16

D_HEAD = 128
HEAD_W = 2 * D_HEAD
CHUNK = 64
N_RWKV = 64
DECAY_LORA = 96
AAA_LORA = 96
GATE_LORA = 256
RMS_EPS = 1e-5
LN_X_EPS = 64e-5
NEG = -1e30

LANES = 128
SUBLANES = 8
VMEM_LIMIT = 56 * 1024 * 1024

LORA_W_OFF, LORA_A_OFF, LORA_G_OFF, LORA_PAD = 0, 128, 256, 512
SEG = 64
PREP_ROWS = 128


def _pick(n, pref, mult):
    best = None
    for d in range(mult, min(n, pref) + 1, mult):
        if n % d == 0:
            best = d
    assert best is not None, (n, pref, mult)
    return best


def _params(sem):
    return pltpu.CompilerParams(dimension_semantics=sem, vmem_limit_bytes=VMEM_LIMIT)


def _rmsnorm_kernel(x_ref, g_ref, o_ref):
    x = x_ref[...]
    ms = jnp.mean(x * x, axis=-1, keepdims=True)
    o_ref[...] = (x * lax.rsqrt(ms + RMS_EPS) * g_ref[...]).astype(o_ref.dtype)


def _rmsnorm(x, g, out_dtype):
    n, d = x.shape
    tr = _pick(n, 256, SUBLANES)
    return pl.pallas_call(
        _rmsnorm_kernel,
        out_shape=jax.ShapeDtypeStruct((n, d), out_dtype),
        grid=(n // tr,),
        in_specs=[pl.BlockSpec((tr, d), lambda i: (i, 0)),
                  pl.BlockSpec((1, d), lambda i: (0, 0))],
        out_specs=pl.BlockSpec((tr, d), lambda i: (i, 0)),
        compiler_params=_params(("parallel",)),
        name="rmsnorm",
    )(x, g.reshape(1, d))


def _mm_kernel(*refs, nk, has_res, has_alias, act, scale):
    refs = list(refs)
    x_ref, w_ref = refs[0], refs[1]
    r_ref = refs[2] if has_res else None
    o_ref = refs[2 + has_res + has_alias]
    acc_ref = refs[-1] if nk > 1 else None

    def epilogue(y):
        if act == "relu2":
            y = jnp.square(jnp.maximum(y, 0.0))
        if scale is not None:
            y = y * scale
        if has_res:
            y = y + r_ref[...]
        o_ref[...] = y.astype(o_ref.dtype).reshape(o_ref.shape)

    part = jnp.dot(x_ref[...], w_ref[...].astype(BF16), preferred_element_type=F32)
    if nk == 1:
        epilogue(part)
        return
    k = pl.program_id(2)

    @pl.when(k == 0)
    def _():
        acc_ref[...] = part

    @pl.when(k > 0)
    def _():
        acc_ref[...] += part

    @pl.when(k == nk - 1)
    def _():
        epilogue(acc_ref[...])


def _matmul(x, w3, layer, *, col0=0, ncols=None, residual=None, act=None, scale=None, out_dtype=F32,
            tm=1024, tn=1024, tk=1024, name="matmul"):
    m, kd = x.shape
    ncols = w3.shape[2] - col0 if ncols is None else ncols
    tm, tn, tk = _pick(m, tm, SUBLANES), _pick(math.gcd(ncols, col0) or ncols, tn, LANES), _pick(kd, tk, LANES)
    nk = kd // tk
    cb = col0 // tn
    in_specs = [pl.BlockSpec((tm, tk), lambda i, j, k: (i, k)),
                pl.BlockSpec((None, tk, tn), lambda i, j, k: (layer, k, cb + j))]
    args = [x, w3]
    if residual is not None:
        in_specs.append(pl.BlockSpec((tm, tn), lambda i, j, k: (i, j)))
        args.append(residual)
    return pl.pallas_call(
        functools.partial(_mm_kernel, nk=nk, has_res=residual is not None, has_alias=False, act=act, scale=scale),
        out_shape=jax.ShapeDtypeStruct((m, ncols), out_dtype),
        grid=(m // tm, ncols // tn, nk),
        in_specs=in_specs,
        out_specs=pl.BlockSpec((tm, tn), lambda i, j, k: (i, j)),
        scratch_shapes=[pltpu.VMEM((tm, tn), F32)] if nk > 1 else [],
        compiler_params=_params(("parallel", "parallel", "arbitrary")),
        name=name,
    )(*args)


def _matmul_stacked(x, w3, layer, *, col0, ncols, row0, rows, depth, prev=None, tm=1024, tn=1024, tk=1024,
                    name="matmul_stacked"):
    m, kd = x.shape
    tm, tn, tk = _pick(math.gcd(rows, row0) or rows, tm, SUBLANES), _pick(math.gcd(ncols, col0), tn, LANES), \
        _pick(kd, tk, LANES)
    nk = kd // tk
    rb, cb = row0 // tm, col0 // tn
    in_specs = [pl.BlockSpec((tm, tk), lambda i, j, k: (rb + i, k)),
                pl.BlockSpec((None, tk, tn), lambda i, j, k: (layer, k, cb + j))]
    args = [x, w3]
    aliases = {}
    if prev is not None:
        in_specs.append(pl.BlockSpec(memory_space=pl.ANY))
        args.append(prev)
        aliases = {2: 0}
    return pl.pallas_call(
        functools.partial(_mm_kernel, nk=nk, has_res=False, has_alias=prev is not None, act=None, scale=None),
        out_shape=jax.ShapeDtypeStruct((depth, rows, ncols), F32),
        grid=(rows // tm, ncols // tn, nk),
        in_specs=in_specs,
        out_specs=pl.BlockSpec((None, tm, tn), lambda i, j, k: (layer, i, j)),
        scratch_shapes=[pltpu.VMEM((tm, tn), F32)] if nk > 1 else [],
        input_output_aliases=aliases,
        compiler_params=_params(("parallel", "parallel", "arbitrary")),
        name=name,
    )(*args)


def _softmax_tile_update(qt_ref, kb, vt, mask, m_ref, l_ref, acc_ref):
    for br in range(2):
        km = kb[:, br * D_HEAD:(br + 1) * D_HEAD]
        st = jnp.dot(km, qt_ref[br * D_HEAD:(br + 1) * D_HEAD, :], preferred_element_type=F32)
        if mask is not None:
            st = jnp.where(mask, st, NEG)
        m_prev = m_ref[br]
        m_new = jnp.maximum(m_prev, jnp.max(st, axis=0, keepdims=True))
        alpha = jnp.exp(m_prev - m_new)
        p = jnp.exp(st - m_new)
        l_ref[br] = alpha * l_ref[br] + jnp.sum(p, axis=0, keepdims=True)
        acc_ref[br] = alpha * acc_ref[br] + jnp.dot(vt, p.astype(BF16), preferred_element_type=F32)
        m_ref[br] = m_new


def _attn_init(q_ref, qt_ref, m_ref, l_ref, acc_ref):
    qt_ref[...] = q_ref[...].astype(F32).T.astype(BF16)
    m_ref[...] = jnp.full(m_ref.shape, NEG, F32)
    l_ref[...] = jnp.zeros(l_ref.shape, F32)
    acc_ref[...] = jnp.zeros(acc_ref.shape, F32)


def _attn_finish(lam_ref, subln_ref, o_ref, l_ref, acc_ref, lam_init):
    lp = lam_ref[...]
    lam = (jnp.exp(jnp.sum(lp[0:1] * lp[1:2], axis=-1, keepdims=True))
           - jnp.exp(jnp.sum(lp[2:3] * lp[3:4], axis=-1, keepdims=True)) + lam_init)
    ot = acc_ref[0] / l_ref[0] - lam * (acc_ref[1] / l_ref[1])
    ot = ot * lax.rsqrt(jnp.mean(ot * ot, axis=0, keepdims=True) + RMS_EPS)
    o_ref[...] = (ot.T * subln_ref[...] * (1.0 - lam_init)).astype(o_ref.dtype)


def _attn_prompt_kernel(lam_ref, subln_ref, q_ref, k_ref, v_ref, mix_ref, kb_ref, vt_ref, qt_ref, m_ref, l_ref,
                        acc_ref, *, tq, lam_init):
    i = pl.program_id(2)
    ntiles = kb_ref.shape[0]

    @pl.when(i == 0)
    def _():
        for j in range(ntiles):
            kb_ref[j] = k_ref[j * tq:(j + 1) * tq, :].astype(BF16)
            vt_ref[j] = v_ref[j * tq:(j + 1) * tq, :].T.astype(BF16)

    _attn_init(q_ref, qt_ref, m_ref, l_ref, acc_ref)

    def body(j, carry):
        _softmax_tile_update(qt_ref, kb_ref[j], vt_ref[j], None, m_ref, l_ref, acc_ref)
        return carry

    lax.fori_loop(0, i, body, 0)
    key_chunk = lax.broadcasted_iota(jnp.int32, (tq, tq), 0) // CHUNK
    qry_chunk = lax.broadcasted_iota(jnp.int32, (tq, tq), 1) // CHUNK
    _softmax_tile_update(qt_ref, kb_ref[i], vt_ref[i], key_chunk <= qry_chunk, m_ref, l_ref, acc_ref)
    _attn_finish(lam_ref, subln_ref, mix_ref, l_ref, acc_ref, lam_init)


def _attn_scratch(tq):
    return [pltpu.VMEM((HEAD_W, tq), BF16),
            pltpu.VMEM((2, 1, tq), F32),
            pltpu.VMEM((2, 1, tq), F32),
            pltpu.VMEM((2, HEAD_W, tq), F32)]


def _attn_prompt(q, k3, v3, layer, lam_p, subln, *, batch, seq, n_rows, d_model, lam_init):
    h_att = k3.shape[2] // HEAD_W
    tq = _pick(seq, 512, CHUNK)
    nq = seq // tq
    kv_spec = pl.BlockSpec((None, seq, HEAD_W), lambda b, h, i: (layer, b, h))
    return pl.pallas_call(
        functools.partial(_attn_prompt_kernel, tq=tq, lam_init=lam_init),
        out_shape=jax.ShapeDtypeStruct((n_rows, d_model), BF16),
        grid=(batch, h_att, nq),
        in_specs=[pl.BlockSpec((4, D_HEAD), lambda b, h, i: (0, 0)),
                  pl.BlockSpec((1, HEAD_W), lambda b, h, i: (0, 0)),
                  pl.BlockSpec((tq, HEAD_W), lambda b, h, i: (b * nq + i, h)),
                  kv_spec, kv_spec],
        out_specs=pl.BlockSpec((tq, HEAD_W), lambda b, h, i: (b * nq + i, h)),
        scratch_shapes=[pltpu.VMEM((nq, tq, HEAD_W), BF16), pltpu.VMEM((nq, HEAD_W, tq), BF16)] + _attn_scratch(tq),
        compiler_params=_params(("parallel", "parallel", "arbitrary")),
        name="attn_prompt",
    )(lam_p, subln, q, k3, v3)


def _softmax_rows_update(qh, kbs, vb, m_ref, l_ref, acc_ref, slot):
    for br in range(2):
        s = lax.dot_general(qh[:, br * D_HEAD:(br + 1) * D_HEAD], kbs[br],
                            (((1,), (1,)), ((), ())), preferred_element_type=F32)
        m_prev = m_ref[slot + br]
        m_new = jnp.maximum(m_prev, jnp.max(s, axis=-1, keepdims=True))
        alpha = jnp.exp(m_prev - m_new)
        p = jnp.exp(s - m_new)
        l_ref[slot + br] = alpha * l_ref[slot + br] + jnp.sum(p, axis=-1, keepdims=True)
        acc_ref[slot + br] = alpha * acc_ref[slot + br] + jnp.dot(p.astype(BF16), vb, preferred_element_type=F32)
        m_ref[slot + br] = m_new


def _attn_sample_kernel(lam_ref, subln_ref, q_ref, kn_ref, vn_ref, k1_ref, k2_ref, v1_ref, v2_ref, mix_in_ref,
                        mix_ref, m_ref, l_ref, acc_ref, *, tp, h_att, nchunks, lam_init):
    del mix_in_ref
    c = pl.program_id(1)

    @pl.when(c == 0)
    def _():
        m_ref[...] = jnp.full(m_ref.shape, NEG, F32)
        l_ref[...] = jnp.zeros(l_ref.shape, F32)
        acc_ref[...] = jnp.zeros(acc_ref.shape, F32)

    def head_rows(ref, h):
        return ref[pl.ds(h, tp, stride=h_att), :].astype(BF16)

    for h in range(h_att):
        vb = jnp.concatenate([head_rows(v1_ref, h), head_rows(v2_ref, h)], axis=1)
        _softmax_rows_update(q_ref[:, h * HEAD_W:(h + 1) * HEAD_W], (head_rows(k1_ref, h), head_rows(k2_ref, h)),
                             vb, m_ref, l_ref, acc_ref, 2 * h)

    @pl.when(c == nchunks - 1)
    def _():
        lp = lam_ref[...]
        lam = (jnp.exp(jnp.sum(lp[0:1] * lp[1:2], axis=-1, keepdims=True))
               - jnp.exp(jnp.sum(lp[2:3] * lp[3:4], axis=-1, keepdims=True)) + lam_init)
        for h in range(h_att):
            cols = slice(h * HEAD_W, (h + 1) * HEAD_W)
            kn = kn_ref[:, cols].astype(BF16)
            _softmax_rows_update(q_ref[:, cols], (kn[:, :D_HEAD], kn[:, D_HEAD:]), vn_ref[:, cols].astype(BF16),
                                 m_ref, l_ref, acc_ref, 2 * h)
            o = acc_ref[2 * h] / l_ref[2 * h] - lam * (acc_ref[2 * h + 1] / l_ref[2 * h + 1])
            o = o * lax.rsqrt(jnp.mean(o * o, axis=-1, keepdims=True) + RMS_EPS)
            mix_ref[:, cols] = (o * subln_ref[...] * (1.0 - lam_init)).astype(mix_ref.dtype)


def _attn_sample(q, k3, v3, cache_k, cache_v, layer, lam_p, subln, mix, *, row0, batch, seq, lam_init):
    h_att = cache_k.shape[3]
    d_att = h_att * HEAD_W
    past = cache_k.shape[2]
    assert past % CHUNK == 0 and seq <= CHUNK and row0 % seq == 0
    rb0 = row0 // seq
    tp = _pick(past, 512, LANES)
    nchunks = past // tp
    new_spec = pl.BlockSpec((None, seq, d_att), lambda b, c: (layer, b, 0))
    cache_k = cache_k.reshape(cache_k.shape[0], batch, past * h_att, HEAD_W)
    cache_v = cache_v.reshape(cache_v.shape[0], batch, past * h_att, HEAD_W)
    past_lo = pl.BlockSpec((None, None, tp * h_att, D_HEAD), lambda b, c: (layer, b, c, 0))
    past_hi = pl.BlockSpec((None, None, tp * h_att, D_HEAD), lambda b, c: (layer, b, c, 1))
    return pl.pallas_call(
        functools.partial(_attn_sample_kernel, tp=tp, h_att=h_att, nchunks=nchunks, lam_init=lam_init),
        out_shape=jax.ShapeDtypeStruct(mix.shape, mix.dtype),
        grid=(batch, nchunks),
        in_specs=[pl.BlockSpec((4, D_HEAD), lambda b, c: (0, 0)),
                  pl.BlockSpec((1, HEAD_W), lambda b, c: (0, 0)),
                  pl.BlockSpec((seq, d_att), lambda b, c: (rb0 + b, 0)),
                  new_spec, new_spec, past_lo, past_hi, past_lo, past_hi,
                  pl.BlockSpec(memory_space=pl.ANY)],
        out_specs=pl.BlockSpec((seq, d_att), lambda b, c: (rb0 + b, 0)),
        input_output_aliases={9: 0},
        scratch_shapes=[pltpu.VMEM((2 * h_att, seq, 1), F32), pltpu.VMEM((2 * h_att, seq, 1), F32),
                        pltpu.VMEM((2 * h_att, seq, HEAD_W), F32)],
        compiler_params=_params(("parallel", "arbitrary")),
        name="attn_sample",
    )(lam_p, subln, q, k3, v3, cache_k, cache_k, cache_v, cache_v, mix)


def _dot_f32(a, b):
    return jnp.dot(a, b, precision=lax.Precision.HIGHEST, preferred_element_type=F32)


def _rwkv_prep_kernel(ur_ref, uk_ref, uv_ref, ul_ref, halo_ref, mu_ref, w0_ref, wup_ref, a0_ref, aup_ref,
                      gup_ref, kkw_ref, kaw_ref, rkw_ref, gsum_ref, gexp_ref,
                      r_out, w_out, k_out, v_out, kk_out, b_out, bonus_out, g_out, *, tt, d_rwkv):
    nseg = tt // SEG
    first_row = lax.broadcasted_iota(jnp.int32, (SEG, 1), 0) == 0

    def shifted(u_ref, col0, width):
        parts = []
        for s in range(nseg):
            u = u_ref[s * SEG:(s + 1) * SEG, :]
            prev = jnp.where(first_row, halo_ref[s:s + 1, col0:col0 + width], pltpu.roll(u, 1, 0))
            parts.append(u + (prev - u) * mu_ref[:, col0:col0 + width])
        return jnp.concatenate(parts, axis=0) if nseg > 1 else parts[0]

    r = shifted(ur_ref, 0, d_rwkv)
    k = shifted(uk_ref, d_rwkv, d_rwkv)
    v = shifted(uv_ref, 2 * d_rwkv, d_rwkv)
    lo = shifted(ul_ref, 3 * d_rwkv, LORA_PAD)
    wd = lo[:, LORA_W_OFF:LORA_A_OFF]
    ad = lo[:, LORA_A_OFF:LORA_G_OFF]
    gd = lo[:, LORA_G_OFF:LORA_PAD]

    w = w0_ref[...] + _dot_f32(jnp.tanh(wd), wup_ref[...])
    x = -w
    w = -(jnp.maximum(x, 0.0) + jnp.log1p(jnp.exp(-jnp.abs(x)))) - 0.5
    decay = jnp.exp(-jnp.exp(w))
    a = jax.nn.sigmoid(a0_ref[...] + _dot_f32(ad, aup_ref[...]))
    g = _dot_f32(jax.nn.sigmoid(gd), gup_ref[...])

    kk = k * kkw_ref[...]
    ss = _dot_f32(kk * kk, gsum_ref[...])
    inv = 1.0 / jnp.maximum(jnp.sqrt(ss), 1e-12)
    kk = kk * _dot_f32(inv, gexp_ref[...])
    kmod = k * (1.0 + (a - 1.0) * kaw_ref[...])
    coef = _dot_f32(r * kmod * rkw_ref[...], gsum_ref[...])

    r_out[...] = r
    w_out[...] = decay
    k_out[...] = kmod
    v_out[...] = v
    kk_out[...] = kk
    b_out[...] = kk * a
    bonus_out[...] = _dot_f32(coef, gexp_ref[...]) * v
    g_out[...] = g


def _rwkv_prep(rkv, lora, halo, mu, w0, wup, a0, aup, gup, kkw, kaw, rkw, gsum, gexp):
    n = rkv.shape[0]
    d_rwkv = rkv.shape[1] // 3
    tt = PREP_ROWS
    assert n % tt == 0
    row = lambda width: pl.BlockSpec((1, width), lambda i: (0, 0))
    full = lambda a: pl.BlockSpec(a.shape, lambda i: (0, 0))
    tile = pl.BlockSpec((tt, d_rwkv), lambda i: (i, 0))
    return pl.pallas_call(
        functools.partial(_rwkv_prep_kernel, tt=tt, d_rwkv=d_rwkv),
        out_shape=[jax.ShapeDtypeStruct((n, d_rwkv), F32)] * 8,
        grid=(n // tt,),
        in_specs=[pl.BlockSpec((tt, d_rwkv), lambda i: (i, 0)),
                  pl.BlockSpec((tt, d_rwkv), lambda i: (i, 1)),
                  pl.BlockSpec((tt, d_rwkv), lambda i: (i, 2)),
                  pl.BlockSpec((tt, LORA_PAD), lambda i: (i, 0)),
                  pl.BlockSpec((None, tt // SEG, halo.shape[-1]), lambda i: (i, 0, 0)),
                  row(mu.shape[-1]), row(d_rwkv), full(wup), row(d_rwkv), full(aup), full(gup),
                  row(d_rwkv), row(d_rwkv), row(d_rwkv), full(gsum), full(gexp)],
        out_specs=[tile] * 8,
        compiler_params=_params(("parallel",)),
        name="rwkv_prep",
    )(rkv, rkv, rkv, lora, halo, mu, w0, wup, a0, aup, gup, kkw, kaw, rkw, gsum, gexp)


def _rwkv_scan_kernel(r_ref, w_ref, k_ref, kk_ref, b_ref, v_ref, s0_ref, o_ref, sout_ref, s_ref, *, tc, nchunks):
    c = pl.program_id(1)
    nkey = s_ref.shape[0]

    @pl.when(c == 0)
    def _():
        s_ref[...] = s0_ref[...]

    def step(t, carry):
        vt = v_ref[t]
        sa = [None] * 4
        for k in range(nkey):
            term = s_ref[k] * kk_ref[t, k:k + 1, :]
            sa[k % 4] = term if sa[k % 4] is None else sa[k % 4] + term
        sa = (sa[0] + sa[1]) + (sa[2] + sa[3])
        oo = [None] * 4
        for k in range(nkey):
            s_new = s_ref[k] * w_ref[t, k:k + 1, :] - sa * b_ref[t, k:k + 1, :] + vt * k_ref[t, k:k + 1, :]
            s_ref[k] = s_new
            term = s_new * r_ref[t, k:k + 1, :]
            oo[k % 4] = term if oo[k % 4] is None else oo[k % 4] + term
        o_ref[t] = (oo[0] + oo[1]) + (oo[2] + oo[3])
        return carry

    lax.fori_loop(0, tc, step, 0)

    @pl.when(c == nchunks - 1)
    def _():
        sout_ref[...] = s_ref[...]


def _rwkv_scan(r, w, k, kk, b, v, s0):
    g, t, nkey, _ = r.shape
    vs = v.shape[2]
    tc = _pick(t, 32, 1)
    nchunks = t // tc
    op = pl.BlockSpec((None, tc, nkey, LANES), lambda gi, c: (gi, c, 0, 0))
    vp = pl.BlockSpec((None, tc, vs, LANES), lambda gi, c: (gi, c, 0, 0))
    sp = pl.BlockSpec((None, nkey, vs, LANES), lambda gi, c: (gi, 0, 0, 0))
    return pl.pallas_call(
        functools.partial(_rwkv_scan_kernel, tc=tc, nchunks=nchunks),
        out_shape=[jax.ShapeDtypeStruct((g, t, vs, LANES), F32),
                   jax.ShapeDtypeStruct((g, nkey, vs, LANES), F32)],
        grid=(g, nchunks),
        in_specs=[op, op, op, op, op, vp, sp],
        out_specs=[vp, sp],
        scratch_shapes=[pltpu.VMEM((nkey, vs, LANES), F32)],
        compiler_params=_params(("parallel", "arbitrary")),
        name="rwkv_scan",
    )(r, w, k, kk, b, v, s0)


def _rwkv_post_kernel(o_ref, bonus_ref, g_ref, lw_ref, lb_ref, gsum_ref, gexp_ref, mix_in_ref, out_ref):
    del mix_in_ref
    o = o_ref[...]
    inv_n = 1.0 / N_RWKV
    mean = _dot_f32(_dot_f32(o, gsum_ref[...]) * inv_n, gexp_ref[...])
    d = o - mean
    var = _dot_f32(d * d, gsum_ref[...]) * inv_n
    rstd = _dot_f32(lax.rsqrt(var + LN_X_EPS), gexp_ref[...])
    y = d * rstd * lw_ref[...] + lb_ref[...]
    out_ref[...] = ((y + bonus_ref[...]) * g_ref[...]).astype(out_ref.dtype)


def _rwkv_post(o, bonus, g, lnx_w, lnx_b, gsum, gexp, mix):
    n, d = o.shape
    assert mix.shape == (n, 2 * d)
    tt = _pick(n, 256, SUBLANES)
    tile = pl.BlockSpec((tt, d), lambda i: (i, 0))
    row = pl.BlockSpec((1, d), lambda i: (0, 0))
    full = lambda a: pl.BlockSpec(a.shape, lambda i: (0, 0))
    return pl.pallas_call(
        _rwkv_post_kernel,
        out_shape=jax.ShapeDtypeStruct(mix.shape, mix.dtype),
        grid=(n // tt,),
        in_specs=[tile, tile, tile, row, row, full(gsum), full(gexp), pl.BlockSpec(memory_space=pl.ANY)],
        out_specs=pl.BlockSpec((tt, d), lambda i: (i, 1)),
        input_output_aliases={7: 0},
        compiler_params=_params(("parallel",)),
        name="rwkv_post",
    )(o, bonus, g, lnx_w, lnx_b, gsum, gexp, mix)


def _lane_plan(batch, h):
    pairs = batch * h
    if pairs >= LANES:
        assert pairs % LANES == 0
        return pairs // LANES, 1
    assert LANES % pairs == 0
    return 1, LANES // pairs


def _to_scan_rows(x, batch, t, h):
    groups, dup = _lane_plan(batch, h)
    y = x.reshape(batch, t, h, N_RWKV).transpose(1, 3, 0, 2).reshape(t, N_RWKV, batch * h)
    if dup > 1:
        return jnp.tile(y, (1, 1, dup))[None]
    return y.reshape(t, N_RWKV, groups, LANES).transpose(2, 0, 1, 3)


def _to_scan_v(x, batch, t, h):
    groups, dup = _lane_plan(batch, h)
    vs = N_RWKV // dup
    y = x.reshape(batch, t, h, dup, vs).transpose(1, 4, 3, 0, 2).reshape(t, vs, dup * batch * h)
    return y.reshape(t, vs, groups, LANES).transpose(2, 0, 1, 3)


def _from_scan_v(o, batch, t, h):
    groups, dup = _lane_plan(batch, h)
    vs = N_RWKV // dup
    y = o.transpose(1, 2, 0, 3).reshape(t, vs, dup, batch, h)
    return y.transpose(3, 0, 4, 2, 1).reshape(batch * t, h * N_RWKV)


def _state_to_scan(s, batch, h):
    groups, dup = _lane_plan(batch, h)
    vs = N_RWKV // dup
    y = s.reshape(batch, h, dup, vs, N_RWKV).transpose(4, 3, 2, 0, 1).reshape(N_RWKV, vs, dup * batch * h)
    return y.reshape(N_RWKV, vs, groups, LANES).transpose(2, 0, 1, 3)


def _state_from_scan(s, batch, h):
    groups, dup = _lane_plan(batch, h)
    vs = N_RWKV // dup
    y = s.transpose(1, 2, 0, 3).reshape(N_RWKV, vs, dup, batch, h)
    return y.transpose(3, 4, 2, 1, 0).reshape(batch, h, N_RWKV, N_RWKV)


def _pad_rwkv_cols(a, d_rwkv):
    z = jnp.zeros(a.shape[:-1] + (LORA_A_OFF - DECAY_LORA,), a.dtype)
    o = 3 * d_rwkv
    return jnp.concatenate([a[..., :o + DECAY_LORA], z,
                            a[..., o + DECAY_LORA:o + DECAY_LORA + AAA_LORA], z,
                            a[..., o + DECAY_LORA + AAA_LORA:]], axis=-1)


def _unpad_rwkv_cols(a, d_rwkv):
    o = 3 * d_rwkv
    return jnp.concatenate([a[..., :o + DECAY_LORA],
                            a[..., o + LORA_A_OFF:o + LORA_A_OFF + AAA_LORA],
                            a[..., o + LORA_G_OFF:]], axis=-1)


def _pad_rows(w, rows):
    return jnp.concatenate([w, jnp.zeros((rows - w.shape[0], w.shape[1]), w.dtype)], axis=0)


def kernel(x_prompt, x_sample, cache_k, cache_v, state_shift, state_wkv, norm_mix, w_in, lam_q1, lam_k1, lam_q2, lam_k2, attn_subln, shift_mu, decay_w0, decay_up, iclr_a0, iclr_up, gate_up, key_kk, key_ka, bonus_rk, lnx_w, lnx_b, w_out, norm_ffn, ffn_up, ffn_down, norm_final):
    bp, tp, d_model = x_prompt.shape
    bs, ts, _ = x_sample.shape
    depth = w_in.shape[0]
    d_att = d_model // 2
    h_att = d_att // HEAD_W
    d_rwkv = d_model - d_att
    h_rwkv = d_rwkv // N_RWKV
    att_cols = 3 * d_att
    n_p, n_s = bp * tp, bs * ts
    n = n_p + n_s
    past = cache_k.shape[2]
    assert tp % SEG == 0 and ts % SEG == 0 and SEG == CHUNK

    x = jnp.concatenate([x_prompt.reshape(n_p, d_model), x_sample.reshape(n_s, d_model)], axis=0)
    rkv_cols = att_cols + 3 * d_rwkv
    w_lora = _pad_rwkv_cols(w_in[:, :, rkv_cols:], 0)

    head_of = jnp.arange(d_rwkv, dtype=jnp.int32) // N_RWKV
    gsum = (head_of[:, None] == jnp.arange(LANES, dtype=jnp.int32)[None, :]).astype(F32)
    gexp = gsum.T

    seg_p = jnp.arange(n_p // SEG) % (tp // SEG) == 0
    seg_s = jnp.arange(n_s // SEG) % (ts // SEG) == 0

    outs = {name: [] for name in ("sp", "wp", "ss", "ws")}
    kp = vp = ks = vs = None
    for l in range(depth):
        lam_init = 0.8 - 0.6 * math.exp(-0.3 * l)
        xn = _rmsnorm(x, norm_mix[l], BF16)
        q = _matmul(xn, w_in, l, col0=0, ncols=d_att, scale=D_HEAD ** -0.5, out_dtype=BF16, name="w_in_q")
        pk = dict(ncols=d_att, row0=0, rows=n_p, depth=depth)
        sk = dict(ncols=d_att, row0=n_p, rows=n_s, depth=depth)
        kp = _matmul_stacked(xn, w_in, l, col0=d_att, prev=kp, name="w_in_k", **pk)
        vp = _matmul_stacked(xn, w_in, l, col0=2 * d_att, prev=vp, name="w_in_v", **pk)
        ks = _matmul_stacked(xn, w_in, l, col0=d_att, prev=ks, name="w_in_ks", **sk)
        vs = _matmul_stacked(xn, w_in, l, col0=2 * d_att, prev=vs, name="w_in_vs", **sk)
        rkv = _matmul(xn, w_in, l, col0=att_cols, ncols=3 * d_rwkv, name="w_in_rkv")
        lora = _matmul(xn, w_lora, l, name="w_in_lora")

        lam_p = jnp.stack([lam_q1[l], lam_k1[l], lam_q2[l], lam_k2[l]])
        subln = attn_subln[l].reshape(1, HEAD_W)
        mix = _attn_prompt(q, kp, vp, l, lam_p, subln, batch=bp, seq=tp, n_rows=n, d_model=d_model,
                           lam_init=lam_init)
        mix = _attn_sample(q, ks, vs, cache_k, cache_v, l, lam_p, subln, mix, row0=n_p, batch=bs, seq=ts,
                           lam_init=lam_init)

        u_last = jnp.concatenate([rkv[SEG - 1::SEG], lora[SEG - 1::SEG]], axis=1)
        prev = jnp.concatenate([jnp.zeros_like(u_last[:1]), u_last[:-1]], axis=0)
        st_p = jnp.zeros((n_p // SEG, u_last.shape[-1]), F32)
        st_s = jnp.repeat(_pad_rwkv_cols(state_shift[l][:, 0, :], d_rwkv), ts // SEG, axis=0)
        start = jnp.concatenate([seg_p, seg_s])[:, None]
        halo = jnp.where(start, jnp.concatenate([st_p, st_s], axis=0), prev)
        halo = halo.reshape(n // PREP_ROWS, PREP_ROWS // SEG, halo.shape[-1])

        mu = _pad_rwkv_cols(shift_mu[l], d_rwkv).reshape(1, -1)
        r, w, k, v, kk, b, bonus, g = _rwkv_prep(
            rkv, lora, halo, mu, decay_w0[l].reshape(1, -1), _pad_rows(decay_up[l], LORA_A_OFF - LORA_W_OFF),
            iclr_a0[l].reshape(1, -1), _pad_rows(iclr_up[l], LORA_G_OFF - LORA_A_OFF), gate_up[l],
            key_kk[l].reshape(1, -1), key_ka[l].reshape(1, -1), bonus_rk[l].reshape(1, -1), gsum, gexp)

        def scan(lo, hi, batch, t, s0):
            rows = [_to_scan_rows(a[lo:hi], batch, t, h_rwkv) for a in (r, w, k, kk, b)]
            o, s_fin = _rwkv_scan(*rows, _to_scan_v(v[lo:hi], batch, t, h_rwkv), _state_to_scan(s0, batch, h_rwkv))
            return _from_scan_v(o, batch, t, h_rwkv), _state_from_scan(s_fin, batch, h_rwkv)

        o_p, wkv_p = scan(0, n_p, bp, tp, jnp.zeros((bp, h_rwkv, N_RWKV, N_RWKV), F32))
        o_s, wkv_s = scan(n_p, n, bs, ts, state_wkv[l])
        mix = _rwkv_post(jnp.concatenate([o_p, o_s], axis=0), bonus, g, lnx_w[l].reshape(1, -1),
                         lnx_b[l].reshape(1, -1), gsum, gexp, mix)

        x = _matmul(mix, w_out, l, residual=x, name="w_out")
        hn = _rmsnorm(x, norm_ffn[l], BF16)
        hid = _matmul(hn, ffn_up, l, act="relu2", out_dtype=BF16, name="ffn_up")
        x = _matmul(hid, ffn_down, l, residual=x, name="ffn_down")

        u_fin = u_last.reshape(-1, 1, u_last.shape[-1])
        outs["sp"].append(_unpad_rwkv_cols(u_fin[tp // SEG - 1:n_p // SEG:tp // SEG], d_rwkv))
        outs["wp"].append(wkv_p)
        outs["ss"].append(_unpad_rwkv_cols(u_fin[n_p // SEG + ts // SEG - 1::ts // SEG], d_rwkv))
        outs["ws"].append(wkv_s)

    y = _rmsnorm(x, norm_final, F32)
    return (y[:n_p].reshape(bp, tp, d_model), y[n_p:].reshape(bs, ts, d_model),
            kp.reshape(depth, bp, tp, h_att, HEAD_W), vp.reshape(depth, bp, tp, h_att, HEAD_W),
            jnp.stack(outs["sp"]), jnp.stack(outs["wp"]),
            ks.reshape(depth, bs, ts, h_att, HEAD_W), vs.reshape(depth, bs, ts, h_att, HEAD_W),
            jnp.stack(outs["ss"]), jnp.stack(outs["ws"]))
```

```python
import functools
import math

import jax
import jax.numpy as jnp
from jax import lax
from jax.experimental import pallas as pl
from jax.experimental.pallas import tpu as pltpu

F32 = jnp.float32
BF16 = jnp.bfloat16

D_HEAD = 128
HEAD_W = 2 * D_HEAD
CHUNK = 64
N_RWKV = 64
DECAY_LORA = 96
AAA_LORA = 96
GATE_LORA = 256
RMS_EPS = 1e-5
LN_X_EPS = 64e-5
NEG = -1e30

LANES = 128
SUBLANES = 8
VMEM_LIMIT = 56 * 1024 * 1024

LORA_W_OFF, LORA_A_OFF, LORA_G_OFF, LORA_PAD = 0, 128, 256, 512
SEG = 64
PREP_ROWS = 128


def _pick(n, pref, mult):
    best = None
    for d in range(mult, min(n, pref) + 1, mult):
        if n % d == 0:
            best = d
    assert best is not None, (n, pref, mult)
    return best


def _params(sem):
    return pltpu.CompilerParams(dimension_semantics=sem, vmem_limit_bytes=VMEM_LIMIT)


def _rmsnorm_kernel(x_ref, g_ref, o_ref):
    x = x_ref[...]
    ms = jnp.mean(x * x, axis=-1, keepdims=True)
    o_ref[...] = (x * lax.rsqrt(ms + RMS_EPS) * g_ref[...]).astype(o_ref.dtype)


def _rmsnorm(x, g, out_dtype):
    n, d = x.shape
    tr = _pick(n, 256, SUBLANES)
    return pl.pallas_call(
        _rmsnorm_kernel,
        out_shape=jax.ShapeDtypeStruct((n, d), out_dtype),
        grid=(n // tr,),
        in_specs=[pl.BlockSpec((tr, d), lambda i: (i, 0)),
                  pl.BlockSpec((1, d), lambda i: (0, 0))],
        out_specs=pl.BlockSpec((tr, d), lambda i: (i, 0)),
        compiler_params=_params(("parallel",)),
        name="rmsnorm",
    )(x, g.reshape(1, d))


def _mm_kernel(*refs, nk, has_res, has_alias, act, scale):
    refs = list(refs)
    x_ref, w_ref = refs[0], refs[1]
    r_ref = refs[2] if has_res else None
    o_ref = refs[2 + has_res + has_alias]
    acc_ref = refs[-1] if nk > 1 else None

    def epilogue(y):
        if act == "relu2":
            y = jnp.square(jnp.maximum(y, 0.0))
        if scale is not None:
            y = y * scale
        if has_res:
            y = y + r_ref[...]
        o_ref[...] = y.astype(o_ref.dtype).reshape(o_ref.shape)

    if nk == 1:
        epilogue(jnp.dot(x_ref[...], w_ref[...].astype(BF16), preferred_element_type=F32))
        return
    k = pl.program_id(2)

    @pl.when(k == 0)
    def _():
        acc_ref[...] = jnp.zeros_like(acc_ref)

    acc_ref[...] += jnp.dot(x_ref[...], w_ref[...].astype(BF16), preferred_element_type=F32)

    @pl.when(k == nk - 1)
    def _():
        epilogue(acc_ref[...])


FULL_K = 4096


def _mm_tiles(kd):
    return (1024, 512, kd) if kd <= FULL_K else (1024, 1024, 2048)


def _matmul(x, w3, layer, *, col0=0, ncols=None, residual=None, act=None, scale=None, out_dtype=F32,
            name="matmul"):
    m, kd = x.shape
    ncols = w3.shape[2] - col0 if ncols is None else ncols
    tm, tn, tk = _mm_tiles(kd)
    tm, tn, tk = _pick(m, tm, SUBLANES), _pick(math.gcd(ncols, col0) or ncols, tn, LANES), _pick(kd, tk, LANES)
    nk = kd // tk
    cb = col0 // tn
    in_specs = [pl.BlockSpec((tm, tk), lambda i, j, k: (i, k)),
                pl.BlockSpec((None, tk, tn), lambda i, j, k: (layer, k, cb + j))]
    args = [x, w3]
    if residual is not None:
        in_specs.append(pl.BlockSpec((tm, tn), lambda i, j, k: (i, j)))
        args.append(residual)
    return pl.pallas_call(
        functools.partial(_mm_kernel, nk=nk, has_res=residual is not None, has_alias=False, act=act, scale=scale),
        out_shape=jax.ShapeDtypeStruct((m, ncols), out_dtype),
        grid=(m // tm, ncols // tn, nk),
        in_specs=in_specs,
        out_specs=pl.BlockSpec((tm, tn), lambda i, j, k: (i, j)),
        scratch_shapes=[pltpu.VMEM((tm, tn), F32)] if nk > 1 else [],
        compiler_params=_params(("parallel", "parallel", "arbitrary")),
        name=name,
    )(*args)


def _matmul_stacked(x, w3, layer, *, col0, ncols, row0, rows, depth, prev=None, name="matmul_stacked"):
    m, kd = x.shape
    tm, tn, tk = _mm_tiles(kd)
    tm, tn, tk = _pick(math.gcd(rows, row0) or rows, tm, SUBLANES), _pick(math.gcd(ncols, col0), tn, LANES), \
        _pick(kd, tk, LANES)
    nk = kd // tk
    rb, cb = row0 // tm, col0 // tn
    in_specs = [pl.BlockSpec((tm, tk), lambda i, j, k: (rb + i, k)),
                pl.BlockSpec((None, tk, tn), lambda i, j, k: (layer, k, cb + j))]
    args = [x, w3]
    aliases = {}
    if prev is not None:
        in_specs.append(pl.BlockSpec(memory_space=pl.ANY))
        args.append(prev)
        aliases = {2: 0}
    return pl.pallas_call(
        functools.partial(_mm_kernel, nk=nk, has_res=False, has_alias=prev is not None, act=None, scale=None),
        out_shape=jax.ShapeDtypeStruct((depth, rows, ncols), F32),
        grid=(rows // tm, ncols // tn, nk),
        in_specs=in_specs,
        out_specs=pl.BlockSpec((None, tm, tn), lambda i, j, k: (layer, i, j)),
        scratch_shapes=[pltpu.VMEM((tm, tn), F32)] if nk > 1 else [],
        input_output_aliases=aliases,
        compiler_params=_params(("parallel", "parallel", "arbitrary")),
        name=name,
    )(*args)


def _softmax_tile_update(qt_ref, kb, vt, mask, m_ref, l_ref, acc_ref):
    for br in range(2):
        km = kb[:, br * D_HEAD:(br + 1) * D_HEAD]
        st = jnp.dot(km, qt_ref[br * D_HEAD:(br + 1) * D_HEAD, :], preferred_element_type=F32)
        if mask is not None:
            st = jnp.where(mask, st, NEG)
        m_prev = m_ref[br]
        m_new = jnp.maximum(m_prev, jnp.max(st, axis=0, keepdims=True))
        alpha = jnp.exp(m_prev - m_new)
        p = jnp.exp(st - m_new)
        l_ref[br] = alpha * l_ref[br] + jnp.sum(p, axis=0, keepdims=True)
        acc_ref[br] = alpha * acc_ref[br] + jnp.dot(vt, p.astype(BF16), preferred_element_type=F32)
        m_ref[br] = m_new


def _attn_init(q_ref, qt_ref, m_ref, l_ref, acc_ref):
    qt_ref[...] = q_ref[...].astype(F32).T.astype(BF16)
    m_ref[...] = jnp.full(m_ref.shape, NEG, F32)
    l_ref[...] = jnp.zeros(l_ref.shape, F32)
    acc_ref[...] = jnp.zeros(acc_ref.shape, F32)


def _attn_finish(lam_ref, subln_ref, o_ref, l_ref, acc_ref, lam_init):
    lp = lam_ref[...]
    lam = (jnp.exp(jnp.sum(lp[0:1] * lp[1:2], axis=-1, keepdims=True))
           - jnp.exp(jnp.sum(lp[2:3] * lp[3:4], axis=-1, keepdims=True)) + lam_init)
    ot = acc_ref[0] / l_ref[0] - lam * (acc_ref[1] / l_ref[1])
    ot = ot * lax.rsqrt(jnp.mean(ot * ot, axis=0, keepdims=True) + RMS_EPS)
    o_ref[...] = (ot.T * subln_ref[...] * (1.0 - lam_init)).astype(o_ref.dtype)


def _attn_prompt_kernel(lam_ref, subln_ref, q_ref, k_ref, v_ref, mix_ref, kb_ref, vt_ref, qt_ref, m_ref, l_ref,
                        acc_ref, *, tq, lam_init):
    i = pl.program_id(2)
    ntiles = kb_ref.shape[0]

    @pl.when(i == 0)
    def _():
        for j in range(ntiles):
            kb_ref[j] = k_ref[j * tq:(j + 1) * tq, :].astype(BF16)
            vt_ref[j] = v_ref[j * tq:(j + 1) * tq, :].T.astype(BF16)

    _attn_init(q_ref, qt_ref, m_ref, l_ref, acc_ref)

    def body(j, carry):
        _softmax_tile_update(qt_ref, kb_ref[j], vt_ref[j], None, m_ref, l_ref, acc_ref)
        return carry

    lax.fori_loop(0, i, body, 0)
    key_chunk = lax.broadcasted_iota(jnp.int32, (tq, tq), 0) // CHUNK
    qry_chunk = lax.broadcasted_iota(jnp.int32, (tq, tq), 1) // CHUNK
    _softmax_tile_update(qt_ref, kb_ref[i], vt_ref[i], key_chunk <= qry_chunk, m_ref, l_ref, acc_ref)
    _attn_finish(lam_ref, subln_ref, mix_ref, l_ref, acc_ref, lam_init)


def _attn_scratch(tq):
    return [pltpu.VMEM((HEAD_W, tq), BF16),
            pltpu.VMEM((2, 1, tq), F32),
            pltpu.VMEM((2, 1, tq), F32),
            pltpu.VMEM((2, HEAD_W, tq), F32)]


def _attn_prompt(q, k3, v3, layer, lam_p, subln, *, batch, seq, n_rows, d_model, lam_init):
    h_att = k3.shape[2] // HEAD_W
    tq = _pick(seq, 512, CHUNK)
    nq = seq // tq
    kv_spec = pl.BlockSpec((None, seq, HEAD_W), lambda b, h, i: (layer, b, h))
    return pl.pallas_call(
        functools.partial(_attn_prompt_kernel, tq=tq, lam_init=lam_init),
        out_shape=jax.ShapeDtypeStruct((n_rows, d_model), BF16),
        grid=(batch, h_att, nq),
        in_specs=[pl.BlockSpec((4, D_HEAD), lambda b, h, i: (0, 0)),
                  pl.BlockSpec((1, HEAD_W), lambda b, h, i: (0, 0)),
                  pl.BlockSpec((tq, HEAD_W), lambda b, h, i: (b * nq + i, h)),
                  kv_spec, kv_spec],
        out_specs=pl.BlockSpec((tq, HEAD_W), lambda b, h, i: (b * nq + i, h)),
        scratch_shapes=[pltpu.VMEM((nq, tq, HEAD_W), BF16), pltpu.VMEM((nq, HEAD_W, tq), BF16)] + _attn_scratch(tq),
        compiler_params=_params(("parallel", "parallel", "arbitrary")),
        name="attn_prompt",
    )(lam_p, subln, q, k3, v3)


def _softmax_rows_update(qh, kbs, vb, m_ref, l_ref, acc_ref, slot):
    for br in range(2):
        s = lax.dot_general(qh[:, br * D_HEAD:(br + 1) * D_HEAD], kbs[br],
                            (((1,), (1,)), ((), ())), preferred_element_type=F32)
        m_prev = m_ref[slot + br]
        m_new = jnp.maximum(m_prev, jnp.max(s, axis=-1, keepdims=True))
        alpha = jnp.exp(m_prev - m_new)
        p = jnp.exp(s - m_new)
        l_ref[slot + br] = alpha * l_ref[slot + br] + jnp.sum(p, axis=-1, keepdims=True)
        acc_ref[slot + br] = alpha * acc_ref[slot + br] + jnp.dot(p.astype(BF16), vb, preferred_element_type=F32)
        m_ref[slot + br] = m_new


def _attn_sample_kernel(lam_ref, subln_ref, q_ref, kn_ref, vn_ref, k1_ref, k2_ref, v1_ref, v2_ref, mix_in_ref,
                        mix_ref, m_ref, l_ref, acc_ref, *, tp, h_att, nchunks, lam_init):
    del mix_in_ref
    c = pl.program_id(1)

    @pl.when(c == 0)
    def _():
        m_ref[...] = jnp.full(m_ref.shape, NEG, F32)
        l_ref[...] = jnp.zeros(l_ref.shape, F32)
        acc_ref[...] = jnp.zeros(acc_ref.shape, F32)

    def head_rows(ref, h):
        return ref[pl.ds(h, tp, stride=h_att), :].astype(BF16)

    for h in range(h_att):
        vb = jnp.concatenate([head_rows(v1_ref, h), head_rows(v2_ref, h)], axis=1)
        _softmax_rows_update(q_ref[:, h * HEAD_W:(h + 1) * HEAD_W], (head_rows(k1_ref, h), head_rows(k2_ref, h)),
                             vb, m_ref, l_ref, acc_ref, 2 * h)

    @pl.when(c == nchunks - 1)
    def _():
        lp = lam_ref[...]
        lam = (jnp.exp(jnp.sum(lp[0:1] * lp[1:2], axis=-1, keepdims=True))
               - jnp.exp(jnp.sum(lp[2:3] * lp[3:4], axis=-1, keepdims=True)) + lam_init)
        for h in range(h_att):
            cols = slice(h * HEAD_W, (h + 1) * HEAD_W)
            kn = kn_ref[:, cols].astype(BF16)
            _softmax_rows_update(q_ref[:, cols], (kn[:, :D_HEAD], kn[:, D_HEAD:]), vn_ref[:, cols].astype(BF16),
                                 m_ref, l_ref, acc_ref, 2 * h)
            o = acc_ref[2 * h] / l_ref[2 * h] - lam * (acc_ref[2 * h + 1] / l_ref[2 * h + 1])
            o = o * lax.rsqrt(jnp.mean(o * o, axis=-1, keepdims=True) + RMS_EPS)
            mix_ref[:, cols] = (o * subln_ref[...] * (1.0 - lam_init)).astype(mix_ref.dtype)


def _attn_sample(q, k3, v3, cache_k, cache_v, layer, lam_p, subln, mix, *, row0, batch, seq, lam_init):
    h_att = cache_k.shape[3]
    d_att = h_att * HEAD_W
    past = cache_k.shape[2]
    assert past % CHUNK == 0 and seq <= CHUNK and row0 % seq == 0
    rb0 = row0 // seq
    tp = _pick(past, 512, LANES)
    nchunks = past // tp
    new_spec = pl.BlockSpec((None, seq, d_att), lambda b, c: (layer, b, 0))
    cache_k = cache_k.reshape(cache_k.shape[0], batch, past * h_att, HEAD_W)
    cache_v = cache_v.reshape(cache_v.shape[0], batch, past * h_att, HEAD_W)
    past_lo = pl.BlockSpec((None, None, tp * h_att, D_HEAD), lambda b, c: (layer, b, c, 0))
    past_hi = pl.BlockSpec((None, None, tp * h_att, D_HEAD), lambda b, c: (layer, b, c, 1))
    return pl.pallas_call(
        functools.partial(_attn_sample_kernel, tp=tp, h_att=h_att, nchunks=nchunks, lam_init=lam_init),
        out_shape=jax.ShapeDtypeStruct(mix.shape, mix.dtype),
        grid=(batch, nchunks),
        in_specs=[pl.BlockSpec((4, D_HEAD), lambda b, c: (0, 0)),
                  pl.BlockSpec((1, HEAD_W), lambda b, c: (0, 0)),
                  pl.BlockSpec((seq, d_att), lambda b, c: (rb0 + b, 0)),
                  new_spec, new_spec, past_lo, past_hi, past_lo, past_hi,
                  pl.BlockSpec(memory_space=pl.ANY)],
        out_specs=pl.BlockSpec((seq, d_att), lambda b, c: (rb0 + b, 0)),
        input_output_aliases={9: 0},
        scratch_shapes=[pltpu.VMEM((2 * h_att, seq, 1), F32), pltpu.VMEM((2 * h_att, seq, 1), F32),
                        pltpu.VMEM((2 * h_att, seq, HEAD_W), F32)],
        compiler_params=_params(("parallel", "arbitrary")),
        name="attn_sample",
    )(lam_p, subln, q, k3, v3, cache_k, cache_k, cache_v, cache_v, mix)


def _dot_f32(a, b):
    return jnp.dot(a, b, precision=lax.Precision.HIGHEST, preferred_element_type=F32)


def _lane_group_sum(x):
    x = x + pltpu.roll(x, 2 * (LANES // 4), 1)
    return x + pltpu.roll(x, LANES // 4, 1)


def _head_sum(x):
    acc = x[:, 0:LANES]
    for j in range(1, x.shape[1] // LANES):
        acc = acc + x[:, j * LANES:(j + 1) * LANES]
    return _lane_group_sum(acc)


def _tile_lanes(slab, width):
    return jnp.concatenate([slab] * (width // LANES), axis=1)


def _rwkv_prep_kernel(ur_ref, uk_ref, uv_ref, ul_ref, halo_ref, mu_ref, w0_ref, wup_ref, a0_ref, aup_ref,
                      gup_ref, kkw_ref, kaw_ref, rkw_ref,
                      r_out, w_out, k_out, vs_out, kk_out, b_out, bonus_out, g_out, *, tt, d_rwkv):
    nseg = tt // SEG
    first_row = lax.broadcasted_iota(jnp.int32, (SEG, 1), 0) == 0

    def shifted(u_ref, col0, width):
        parts = []
        for s in range(nseg):
            u = u_ref[s * SEG:(s + 1) * SEG, :]
            prev = jnp.where(first_row, halo_ref[s:s + 1, col0:col0 + width], pltpu.roll(u, 1, 0))
            parts.append(u + (prev - u) * mu_ref[:, col0:col0 + width])
        return jnp.concatenate(parts, axis=0) if nseg > 1 else parts[0]

    r = shifted(ur_ref, 0, d_rwkv)
    k = shifted(uk_ref, d_rwkv, d_rwkv)
    v = shifted(uv_ref, 2 * d_rwkv, d_rwkv)
    lo = shifted(ul_ref, 3 * d_rwkv, LORA_PAD)
    wd = lo[:, LORA_W_OFF:LORA_A_OFF]
    ad = lo[:, LORA_A_OFF:LORA_G_OFF]
    gd = lo[:, LORA_G_OFF:LORA_PAD]

    w = w0_ref[...] + _dot_f32(jnp.tanh(wd), wup_ref[...])
    x = -w
    w = -(jnp.maximum(x, 0.0) + jnp.log1p(jnp.exp(-jnp.abs(x)))) - 0.5
    decay = jnp.exp(-jnp.exp(w))
    a = jax.nn.sigmoid(a0_ref[...] + _dot_f32(ad, aup_ref[...]))
    g = _dot_f32(jax.nn.sigmoid(gd), gup_ref[...])

    kk = k * kkw_ref[...]
    inv = 1.0 / jnp.maximum(jnp.sqrt(_head_sum(kk * kk)), 1e-12)
    kk = kk * _tile_lanes(inv, d_rwkv)
    kmod = k * (1.0 + (a - 1.0) * kaw_ref[...])
    coef = _head_sum(r * kmod * rkw_ref[...])

    r_out[...] = r
    w_out[...] = decay
    k_out[...] = kmod
    kk_out[...] = kk
    b_out[...] = kk * a
    bonus_out[...] = _tile_lanes(coef, d_rwkv) * v
    g_out[...] = g

    group = lax.broadcasted_iota(jnp.int32, (tt, LANES), 1) // (LANES // 4)
    for j in range(d_rwkv // LANES):
        slab = v[:, j * LANES:(j + 1) * LANES]
        for gi in range(4):
            vs_out[pl.ds(4 * j + gi, tt, stride=N_RWKV), :] = _lane_group_sum(jnp.where(group == gi, slab, 0.0))


def _rwkv_prep(rkv, lora, halo, mu, w0, wup, a0, aup, gup, kkw, kaw, rkw):
    n = rkv.shape[0]
    d_rwkv = rkv.shape[1] // 3
    tt = PREP_ROWS
    assert n % tt == 0 and d_rwkv // N_RWKV == LANES // 4
    row = lambda width: pl.BlockSpec((1, width), lambda i: (0, 0))
    full = lambda a: pl.BlockSpec(a.shape, lambda i: (0, 0))
    tile = pl.BlockSpec((tt, d_rwkv), lambda i: (i, 0))
    nat = jax.ShapeDtypeStruct((n, d_rwkv), F32)
    return pl.pallas_call(
        functools.partial(_rwkv_prep_kernel, tt=tt, d_rwkv=d_rwkv),
        out_shape=[nat, nat, nat, jax.ShapeDtypeStruct((n * N_RWKV, LANES), F32), nat, nat, nat, nat],
        grid=(n // tt,),
        in_specs=[pl.BlockSpec((tt, d_rwkv), lambda i: (i, 0)),
                  pl.BlockSpec((tt, d_rwkv), lambda i: (i, 1)),
                  pl.BlockSpec((tt, d_rwkv), lambda i: (i, 2)),
                  pl.BlockSpec((tt, LORA_PAD), lambda i: (i, 0)),
                  pl.BlockSpec((None, tt // SEG, halo.shape[-1]), lambda i: (i, 0, 0)),
                  row(mu.shape[-1]), row(d_rwkv), full(wup), row(d_rwkv), full(aup), full(gup),
                  row(d_rwkv), row(d_rwkv), row(d_rwkv)],
        out_specs=[tile, tile, tile, pl.BlockSpec((tt * N_RWKV, LANES), lambda i: (i, 0)), tile, tile, tile, tile],
        compiler_params=_params(("parallel",)),
        name="rwkv_prep",
    )(rkv, rkv, rkv, lora, halo, mu, w0, wup, a0, aup, gup, kkw, kaw, rkw)


SCAN_SEQS = 2


def _rwkv_scan_kernel(*refs, tc, nchunks):
    nops = 5
    op_refs = [refs[e * nops:(e + 1) * nops] for e in range(SCAN_SEQS)]
    v_refs = refs[SCAN_SEQS * nops:SCAN_SEQS * (nops + 1)]
    s0_ref, o_ref, sout_ref, s_ref = refs[SCAN_SEQS * (nops + 1):]
    c = pl.program_id(1)
    nslab = s_ref.shape[1]

    @pl.when(c == 0)
    def _():
        s_ref[...] = s0_ref[...]

    def token(t8, i, e):
        base = pl.multiple_of(t8 * SUBLANES, SUBLANES)
        r_ref, w_ref, k_ref, kk_ref, b_ref = (ref.at[pl.ds(base, SUBLANES), :] for ref in op_refs[e])

        def row(ref, j):
            return ref[i:i + 1, j * LANES:(j + 1) * LANES]

        vt = v_refs[e][base + i]
        sa = [None] * 4
        for j in range(nslab):
            term = s_ref[e, j] * row(kk_ref, j)
            sa[j % 4] = term if sa[j % 4] is None else sa[j % 4] + term
        sa = _lane_group_sum((sa[0] + sa[1]) + (sa[2] + sa[3]))
        oo = [None] * 4
        for j in range(nslab):
            s_new = s_ref[e, j] * row(w_ref, j) - sa * row(b_ref, j) + vt * row(k_ref, j)
            s_ref[e, j] = s_new
            term = s_new * row(r_ref, j)
            oo[j % 4] = term if oo[j % 4] is None else oo[j % 4] + term
        o_ref[e, base + i] = _lane_group_sum((oo[0] + oo[1]) + (oo[2] + oo[3]))

    def step8(t8, carry):
        for i in range(SUBLANES):
            for e in range(SCAN_SEQS):
                token(t8, i, e)
        return carry

    lax.fori_loop(0, tc // SUBLANES, step8, 0)

    @pl.when(c == nchunks - 1)
    def _():
        sout_ref[...] = s_ref[...]


def _rwkv_scan(r, w, k, kk, b, v_scan, s0, *, row0, batch, seq):
    d_rwkv = r.shape[1]
    nslab = d_rwkv // LANES
    tc = _pick(seq, 32, SUBLANES)
    nchunks = seq // tc
    assert batch % SCAN_SEQS == 0 and row0 % tc == 0
    op_specs, v_specs = [], []
    for e in range(SCAN_SEQS):
        first = lambda g, c, e=e: (row0 + (SCAN_SEQS * g + e) * seq) // tc + c
        op_specs += [pl.BlockSpec((tc, d_rwkv), lambda g, c, f=first: (f(g, c), 0))] * 5
        v_specs.append(pl.BlockSpec((tc, N_RWKV, LANES), lambda g, c, f=first: (f(g, c), 0, 0)))
    state_spec = pl.BlockSpec((SCAN_SEQS, nslab, N_RWKV, LANES), lambda g, c: (g, 0, 0, 0))
    return pl.pallas_call(
        functools.partial(_rwkv_scan_kernel, tc=tc, nchunks=nchunks),
        out_shape=[jax.ShapeDtypeStruct((batch, seq, N_RWKV, LANES), F32),
                   jax.ShapeDtypeStruct((batch, nslab, N_RWKV, LANES), F32)],
        grid=(batch // SCAN_SEQS, nchunks),
        in_specs=op_specs + v_specs + [state_spec],
        out_specs=[pl.BlockSpec((SCAN_SEQS, tc, N_RWKV, LANES), lambda g, c: (g, c, 0, 0)), state_spec],
        scratch_shapes=[pltpu.VMEM((SCAN_SEQS, nslab, N_RWKV, LANES), F32)],
        compiler_params=_params(("parallel", "arbitrary")),
        name="rwkv_scan",
    )(*([r, w, k, kk, b] * SCAN_SEQS), *([v_scan] * SCAN_SEQS), s0)


def _rwkv_post_kernel(o_ref, bonus_ref, g_ref, lw_ref, lb_ref, mix_in_ref, out_ref, *, tt):
    del mix_in_ref
    d = bonus_ref.shape[1]
    group = lax.broadcasted_iota(jnp.int32, (tt, LANES), 1) // (LANES // 4)
    slabs = []
    for j in range(d // LANES):
        ch = [o_ref[pl.ds(4 * j + gi, tt, stride=N_RWKV), :] for gi in range(4)]
        slabs.append(jnp.where(group == 0, ch[0], jnp.where(group == 1, ch[1], jnp.where(group == 2, ch[2], ch[3]))))
    o = jnp.concatenate(slabs, axis=1)
    inv_n = 1.0 / N_RWKV
    dev = o - _tile_lanes(_head_sum(o) * inv_n, d)
    rstd = lax.rsqrt(_head_sum(dev * dev) * inv_n + LN_X_EPS)
    y = dev * _tile_lanes(rstd, d) * lw_ref[...] + lb_ref[...]
    out_ref[...] = ((y + bonus_ref[...]) * g_ref[...]).astype(out_ref.dtype)


def _rwkv_post(o_scan, bonus, g, lnx_w, lnx_b, mix, *, row0):
    rows = o_scan.shape[0] // N_RWKV
    d = bonus.shape[1]
    tt = _pick(math.gcd(rows, row0) or rows, 256, SUBLANES)
    rb = row0 // tt
    tile = pl.BlockSpec((tt, d), lambda i: (rb + i, 0))
    row = pl.BlockSpec((1, d), lambda i: (0, 0))
    return pl.pallas_call(
        functools.partial(_rwkv_post_kernel, tt=tt),
        out_shape=jax.ShapeDtypeStruct(mix.shape, mix.dtype),
        grid=(rows // tt,),
        in_specs=[pl.BlockSpec((tt * N_RWKV, LANES), lambda i: (i, 0)), tile, tile, row, row,
                  pl.BlockSpec(memory_space=pl.ANY)],
        out_specs=pl.BlockSpec((tt, d), lambda i: (rb + i, 1)),
        input_output_aliases={5: 0},
        compiler_params=_params(("parallel",)),
        name="rwkv_post",
    )(o_scan, bonus, g, lnx_w, lnx_b, mix)


def _channel_major(a, axis=-1):
    axis = axis % a.ndim
    h = a.shape[axis] // N_RWKV
    y = a.reshape(a.shape[:axis] + (h, N_RWKV) + a.shape[axis + 1:])
    return jnp.swapaxes(y, axis, axis + 1).reshape(a.shape)


def _head_major(a, axis=-1):
    axis = axis % a.ndim
    h = a.shape[axis] // N_RWKV
    y = a.reshape(a.shape[:axis] + (N_RWKV, h) + a.shape[axis + 1:])
    return jnp.swapaxes(y, axis, axis + 1).reshape(a.shape)


def _state_to_scan(s):
    b, h, nv, nk = s.shape
    return s.reshape(b, h, nv, nk // 4, 4).transpose(0, 3, 2, 4, 1).reshape(b, nk // 4, nv, 4 * h)


def _state_from_scan(s):
    b, nslab, nv, lanes = s.shape
    h = lanes // 4
    return s.reshape(b, nslab, nv, 4, h).transpose(0, 4, 2, 1, 3).reshape(b, h, nv, nslab * 4)


def _rkv_channel_major(a, d_rwkv):
    parts = [_channel_major(a[..., i * d_rwkv:(i + 1) * d_rwkv]) for i in range(3)]
    return jnp.concatenate(parts + [a[..., 3 * d_rwkv:]], axis=-1)


def _rkv_head_major(a, d_rwkv):
    parts = [_head_major(a[..., i * d_rwkv:(i + 1) * d_rwkv]) for i in range(3)]
    return jnp.concatenate(parts + [a[..., 3 * d_rwkv:]], axis=-1)


def _pad_rwkv_cols(a, d_rwkv):
    z = jnp.zeros(a.shape[:-1] + (LORA_A_OFF - DECAY_LORA,), a.dtype)
    o = 3 * d_rwkv
    return jnp.concatenate([a[..., :o + DECAY_LORA], z,
                            a[..., o + DECAY_LORA:o + DECAY_LORA + AAA_LORA], z,
                            a[..., o + DECAY_LORA + AAA_LORA:]], axis=-1)


def _unpad_rwkv_cols(a, d_rwkv):
    o = 3 * d_rwkv
    return jnp.concatenate([a[..., :o + DECAY_LORA],
                            a[..., o + LORA_A_OFF:o + LORA_A_OFF + AAA_LORA],
                            a[..., o + LORA_G_OFF:]], axis=-1)


def _pad_rows(w, rows):
    return jnp.concatenate([w, jnp.zeros((rows - w.shape[0], w.shape[1]), w.dtype)], axis=0)


def kernel(x_prompt, x_sample, cache_k, cache_v, state_shift, state_wkv, norm_mix, w_in, lam_q1, lam_k1, lam_q2, lam_k2, attn_subln, shift_mu, decay_w0, decay_up, iclr_a0, iclr_up, gate_up, key_kk, key_ka, bonus_rk, lnx_w, lnx_b, w_out, norm_ffn, ffn_up, ffn_down, norm_final):
    bp, tp, d_model = x_prompt.shape
    bs, ts, _ = x_sample.shape
    depth = w_in.shape[0]
    d_att = d_model // 2
    h_att = d_att // HEAD_W
    d_rwkv = d_model - d_att
    h_rwkv = d_rwkv // N_RWKV
    att_cols = 3 * d_att
    n_p, n_s = bp * tp, bs * ts
    n = n_p + n_s
    past = cache_k.shape[2]
    assert tp % SEG == 0 and ts % SEG == 0 and SEG == CHUNK

    x = jnp.concatenate([x_prompt.reshape(n_p, d_model), x_sample.reshape(n_s, d_model)], axis=0)
    rkv_cols = att_cols + 3 * d_rwkv
    w_lora = _pad_rwkv_cols(w_in[:, :, rkv_cols:], 0)
    w_rkv = _rkv_channel_major(w_in[:, :, att_cols:rkv_cols], d_rwkv)
    w_out_cm = jnp.concatenate([w_out[:, :d_att], _channel_major(w_out[:, d_att:], axis=1)], axis=1)
    row_cm = lambda a: _channel_major(a.reshape(-1)).reshape(1, -1)

    seg_p = jnp.arange(n_p // SEG) % (tp // SEG) == 0
    seg_s = jnp.arange(n_s // SEG) % (ts // SEG) == 0

    outs = {name: [] for name in ("sp", "wp", "ss", "ws")}
    kp = vp = ks = vs = None
    for l in range(depth):
        lam_init = 0.8 - 0.6 * math.exp(-0.3 * l)
        xn = _rmsnorm(x, norm_mix[l], BF16)
        q = _matmul(xn, w_in, l, col0=0, ncols=d_att, scale=D_HEAD ** -0.5, out_dtype=BF16, name="w_in_q")
        pk = dict(ncols=d_att, row0=0, rows=n_p, depth=depth)
        sk = dict(ncols=d_att, row0=n_p, rows=n_s, depth=depth)
        kp = _matmul_stacked(xn, w_in, l, col0=d_att, prev=kp, name="w_in_k", **pk)
        vp = _matmul_stacked(xn, w_in, l, col0=2 * d_att, prev=vp, name="w_in_v", **pk)
        ks = _matmul_stacked(xn, w_in, l, col0=d_att, prev=ks, name="w_in_ks", **sk)
        vs = _matmul_stacked(xn, w_in, l, col0=2 * d_att, prev=vs, name="w_in_vs", **sk)
        rkv = _matmul(xn, w_rkv, l, name="w_in_rkv")
        lora = _matmul(xn, w_lora, l, name="w_in_lora")

        lam_p = jnp.stack([lam_q1[l], lam_k1[l], lam_q2[l], lam_k2[l]])
        subln = attn_subln[l].reshape(1, HEAD_W)
        mix = _attn_prompt(q, kp, vp, l, lam_p, subln, batch=bp, seq=tp, n_rows=n, d_model=d_model,
                           lam_init=lam_init)
        mix = _attn_sample(q, ks, vs, cache_k, cache_v, l, lam_p, subln, mix, row0=n_p, batch=bs, seq=ts,
                           lam_init=lam_init)

        u_last = jnp.concatenate([rkv[SEG - 1::SEG], lora[SEG - 1::SEG]], axis=1)
        prev = jnp.concatenate([jnp.zeros_like(u_last[:1]), u_last[:-1]], axis=0)
        st_p = jnp.zeros((n_p // SEG, u_last.shape[-1]), F32)
        shift_in = _rkv_channel_major(_pad_rwkv_cols(state_shift[l][:, 0, :], d_rwkv), d_rwkv)
        st_s = jnp.repeat(shift_in, ts // SEG, axis=0)
        start = jnp.concatenate([seg_p, seg_s])[:, None]
        halo = jnp.where(start, jnp.concatenate([st_p, st_s], axis=0), prev)
        halo = halo.reshape(n // PREP_ROWS, PREP_ROWS // SEG, halo.shape[-1])

        mu = _rkv_channel_major(_pad_rwkv_cols(shift_mu[l], d_rwkv), d_rwkv).reshape(1, -1)
        r, w, k, v_scan, kk, b, bonus, g = _rwkv_prep(
            rkv, lora, halo, mu,
            row_cm(decay_w0[l]), _pad_rows(_channel_major(decay_up[l]), LORA_A_OFF - LORA_W_OFF),
            row_cm(iclr_a0[l]), _pad_rows(_channel_major(iclr_up[l]), LORA_G_OFF - LORA_A_OFF),
            _channel_major(gate_up[l]), row_cm(key_kk[l]), row_cm(key_ka[l]), row_cm(bonus_rk[l]))
        v_scan = v_scan.reshape(n, N_RWKV, LANES)

        scan_ops = (r, w, k, kk, b, v_scan)
        s0_p = jnp.zeros((bp, d_rwkv // LANES, N_RWKV, LANES), F32)
        o_p, s_p = _rwkv_scan(*scan_ops, s0_p, row0=0, batch=bp, seq=tp)
        o_s, s_s = _rwkv_scan(*scan_ops, _state_to_scan(state_wkv[l]), row0=n_p, batch=bs, seq=ts)
        post_w = (row_cm(lnx_w[l]), row_cm(lnx_b[l]))
        mix = _rwkv_post(o_p.reshape(n_p * N_RWKV, LANES), bonus, g, *post_w, mix, row0=0)
        mix = _rwkv_post(o_s.reshape(n_s * N_RWKV, LANES), bonus, g, *post_w, mix, row0=n_p)

        x = _matmul(mix, w_out_cm, l, residual=x, name="w_out")
        hn = _rmsnorm(x, norm_ffn[l], BF16)
        hid = _matmul(hn, ffn_up, l, act="relu2", out_dtype=BF16, name="ffn_up")
        x = _matmul(hid, ffn_down, l, residual=x, name="ffn_down")

        u_fin = _rkv_head_major(u_last, d_rwkv).reshape(-1, 1, u_last.shape[-1])
        outs["sp"].append(_unpad_rwkv_cols(u_fin[tp // SEG - 1:n_p // SEG:tp // SEG], d_rwkv))
        outs["wp"].append(_state_from_scan(s_p))
        outs["ss"].append(_unpad_rwkv_cols(u_fin[n_p // SEG + ts // SEG - 1::ts // SEG], d_rwkv))
        outs["ws"].append(_state_from_scan(s_s))

    y = _rmsnorm(x, norm_final, F32)
    return (y[:n_p].reshape(bp, tp, d_model), y[n_p:].reshape(bs, ts, d_model),
            kp.reshape(depth, bp, tp, h_att, HEAD_W), vp.reshape(depth, bp, tp, h_att, HEAD_W),
            jnp.stack(outs["sp"]), jnp.stack(outs["wp"]),
            ks.reshape(depth, bs, ts, h_att, HEAD_W), vs.reshape(depth, bs, ts, h_att, HEAD_W),
            jnp.stack(outs["ss"]), jnp.stack(outs["ws"]))
```

```python
import functools
import math

import jax
import jax.numpy as jnp
from jax import lax
from jax.experimental import pallas as pl
from jax.experimental.pallas import tpu as pltpu

F32 = jnp.float32
BF16 = jnp.bfloat16

D_HEAD = 128
HEAD_W = 2 * D_HEAD
CHUNK = 64
N_RWKV = 64
DECAY_LORA = 96
AAA_LORA = 96
GATE_LORA = 256
RMS_EPS = 1e-5
LN_X_EPS = 64e-5
NEG = -1e30

LANES = 128
SUBLANES = 8
VMEM_LIMIT = 56 * 1024 * 1024

LORA_W_OFF, LORA_A_OFF, LORA_G_OFF, LORA_PAD = 0, 128, 256, 512
SEG = 64
PREP_ROWS = 128


def _pick(n, pref, mult):
    best = None
    for d in range(mult, min(n, pref) + 1, mult):
        if n % d == 0:
            best = d
    assert best is not None, (n, pref, mult)
    return best


def _params(sem, flags=None):
    return pltpu.CompilerParams(dimension_semantics=sem, vmem_limit_bytes=VMEM_LIMIT, flags=flags)


def _rmsnorm_kernel(x_ref, g_ref, o_ref):
    x = x_ref[...]
    ms = jnp.mean(x * x, axis=-1, keepdims=True)
    o_ref[...] = (x * lax.rsqrt(ms + RMS_EPS) * g_ref[...]).astype(o_ref.dtype)


def _rmsnorm(x, g, out_dtype, row0=0, rows=None):
    n, d = x.shape
    rows = n - row0 if rows is None else rows
    tr = _pick(math.gcd(rows, row0) or rows, 256, SUBLANES)
    rb = row0 // tr
    return pl.pallas_call(
        _rmsnorm_kernel,
        out_shape=jax.ShapeDtypeStruct((rows, d), out_dtype),
        grid=(rows // tr,),
        in_specs=[pl.BlockSpec((tr, d), lambda i: (rb + i, 0)),
                  pl.BlockSpec((1, d), lambda i: (0, 0))],
        out_specs=pl.BlockSpec((tr, d), lambda i: (i, 0)),
        compiler_params=_params(("parallel",)),
        name="rmsnorm",
    )(x, g.reshape(1, d))


def _mm_kernel(*refs, nk, has_res, has_alias, act, scale):
    refs = list(refs)
    x_ref, w_ref = refs[0], refs[1]
    r_ref = refs[2] if has_res else None
    o_ref = refs[2 + has_res + has_alias]
    acc_ref = refs[-1] if nk > 1 else None

    def epilogue(y):
        if act == "relu2":
            y = jnp.square(jnp.maximum(y, 0.0))
        if scale is not None:
            y = y * scale
        if has_res:
            y = y + r_ref[...]
        o_ref[...] = y.astype(o_ref.dtype).reshape(o_ref.shape)

    if nk == 1:
        epilogue(jnp.dot(x_ref[...], w_ref[...].astype(BF16), preferred_element_type=F32))
        return
    k = pl.program_id(2)

    @pl.when(k == 0)
    def _():
        acc_ref[...] = jnp.zeros_like(acc_ref)

    acc_ref[...] += jnp.dot(x_ref[...], w_ref[...].astype(BF16), preferred_element_type=F32)

    @pl.when(k == nk - 1)
    def _():
        epilogue(acc_ref[...])


FULL_K = 4096


def _mm_tiles(kd):
    return (1024, 512, kd) if kd <= FULL_K else (1024, 1024, 2048)


def _matmul(x, w3, layer, *, col0=0, ncols=None, residual=None, act=None, scale=None, out_dtype=F32,
            name="matmul"):
    m, kd = x.shape
    ncols = w3.shape[2] - col0 if ncols is None else ncols
    tm, tn, tk = _mm_tiles(kd)
    tm, tn, tk = _pick(m, tm, SUBLANES), _pick(math.gcd(ncols, col0) or ncols, tn, LANES), _pick(kd, tk, LANES)
    nk = kd // tk
    cb = col0 // tn
    in_specs = [pl.BlockSpec((tm, tk), lambda i, j, k: (i, k)),
                pl.BlockSpec((None, tk, tn), lambda i, j, k: (layer, k, cb + j))]
    args = [x, w3]
    if residual is not None:
        in_specs.append(pl.BlockSpec((tm, tn), lambda i, j, k: (i, j)))
        args.append(residual)
    return pl.pallas_call(
        functools.partial(_mm_kernel, nk=nk, has_res=residual is not None, has_alias=False, act=act, scale=scale),
        out_shape=jax.ShapeDtypeStruct((m, ncols), out_dtype),
        grid=(m // tm, ncols // tn, nk),
        in_specs=in_specs,
        out_specs=pl.BlockSpec((tm, tn), lambda i, j, k: (i, j)),
        scratch_shapes=[pltpu.VMEM((tm, tn), F32)] if nk > 1 else [],
        compiler_params=_params(("parallel", "parallel", "arbitrary")),
        name=name,
    )(*args)


def _matmul_stacked(x, w3, layer, stack, *, col0, row0, name="matmul_stacked"):
    m, kd = x.shape
    depth, rows, ncols = stack.shape
    tm, tn, tk = _mm_tiles(kd)
    tm, tn, tk = _pick(math.gcd(rows, row0) or rows, tm, SUBLANES), _pick(math.gcd(ncols, col0), tn, LANES), \
        _pick(kd, tk, LANES)
    nk = kd // tk
    rb, cb = row0 // tm, col0 // tn
    return pl.pallas_call(
        functools.partial(_mm_kernel, nk=nk, has_res=False, has_alias=True, act=None, scale=None),
        out_shape=jax.ShapeDtypeStruct(stack.shape, stack.dtype),
        grid=(rows // tm, ncols // tn, nk),
        in_specs=[pl.BlockSpec((tm, tk), lambda i, j, k: (rb + i, k)),
                  pl.BlockSpec((None, tk, tn), lambda i, j, k: (layer, k, cb + j)),
                  pl.BlockSpec(memory_space=pl.ANY)],
        out_specs=pl.BlockSpec((None, tm, tn), lambda i, j, k: (layer, i, j)),
        scratch_shapes=[pltpu.VMEM((tm, tn), F32)] if nk > 1 else [],
        input_output_aliases={2: 0},
        compiler_params=_params(("parallel", "parallel", "arbitrary")),
        name=name,
    )(x, w3, stack)


def _softmax_tile_update(qt_ref, kb, vt, mask, m_ref, l_ref, acc_ref):
    for br in range(2):
        km = kb[:, br * D_HEAD:(br + 1) * D_HEAD]
        st = jnp.dot(km, qt_ref[br * D_HEAD:(br + 1) * D_HEAD, :], preferred_element_type=F32)
        if mask is not None:
            st = jnp.where(mask, st, NEG)
        m_prev = m_ref[br]
        m_new = jnp.maximum(m_prev, jnp.max(st, axis=0, keepdims=True))
        alpha = jnp.exp(m_prev - m_new)
        p = jnp.exp(st - m_new)
        l_ref[br] = alpha * l_ref[br] + jnp.sum(p, axis=0, keepdims=True)
        acc_ref[br] = alpha * acc_ref[br] + jnp.dot(vt, p.astype(BF16), preferred_element_type=F32)
        m_ref[br] = m_new


def _attn_init(q_ref, qt_ref, m_ref, l_ref, acc_ref):
    qt_ref[...] = q_ref[...].astype(F32).T.astype(BF16)
    m_ref[...] = jnp.full(m_ref.shape, NEG, F32)
    l_ref[...] = jnp.zeros(l_ref.shape, F32)
    acc_ref[...] = jnp.zeros(acc_ref.shape, F32)


def _attn_finish(lam_ref, subln_ref, o_ref, l_ref, acc_ref, lam_init):
    lp = lam_ref[...]
    lam = (jnp.exp(jnp.sum(lp[0:1] * lp[1:2], axis=-1, keepdims=True))
           - jnp.exp(jnp.sum(lp[2:3] * lp[3:4], axis=-1, keepdims=True)) + lam_init)
    ot = acc_ref[0] / l_ref[0] - lam * (acc_ref[1] / l_ref[1])
    ot = ot * lax.rsqrt(jnp.mean(ot * ot, axis=0, keepdims=True) + RMS_EPS)
    o_ref[...] = (ot.T * subln_ref[...] * (1.0 - lam_init)).astype(o_ref.dtype)


def _attn_prompt_kernel(lam_ref, subln_ref, q_ref, k_ref, v_ref, mix_in_ref, mix_ref, kb_ref, vt_ref, qt_ref,
                        m_ref, l_ref, acc_ref, *, tq, lam_init):
    del mix_in_ref
    i = pl.program_id(2)
    ntiles = kb_ref.shape[0]

    @pl.when(i == 0)
    def _():
        for j in range(ntiles):
            kb_ref[j] = k_ref[j * tq:(j + 1) * tq, :].astype(BF16)
            vt_ref[j] = v_ref[j * tq:(j + 1) * tq, :].T.astype(BF16)

    _attn_init(q_ref, qt_ref, m_ref, l_ref, acc_ref)

    def body(j, carry):
        _softmax_tile_update(qt_ref, kb_ref[j], vt_ref[j], None, m_ref, l_ref, acc_ref)
        return carry

    lax.fori_loop(0, i, body, 0)
    key_chunk = lax.broadcasted_iota(jnp.int32, (tq, tq), 0) // CHUNK
    qry_chunk = lax.broadcasted_iota(jnp.int32, (tq, tq), 1) // CHUNK
    _softmax_tile_update(qt_ref, kb_ref[i], vt_ref[i], key_chunk <= qry_chunk, m_ref, l_ref, acc_ref)
    _attn_finish(lam_ref, subln_ref, mix_ref, l_ref, acc_ref, lam_init)


def _attn_scratch(tq):
    return [pltpu.VMEM((HEAD_W, tq), BF16),
            pltpu.VMEM((2, 1, tq), F32),
            pltpu.VMEM((2, 1, tq), F32),
            pltpu.VMEM((2, HEAD_W, tq), F32)]


def _attn_prompt(q, k3, v3, layer, lam_p, subln, mix, *, batch, seq, lam_init):
    h_att = k3.shape[2] // HEAD_W
    tq = _pick(seq, 512, CHUNK)
    nq = seq // tq
    kv_spec = pl.BlockSpec((None, seq, HEAD_W), lambda b, h, i: (layer, b, h))
    return pl.pallas_call(
        functools.partial(_attn_prompt_kernel, tq=tq, lam_init=lam_init),
        out_shape=jax.ShapeDtypeStruct(mix.shape, mix.dtype),
        grid=(batch, h_att, nq),
        in_specs=[pl.BlockSpec((4, D_HEAD), lambda b, h, i: (0, 0)),
                  pl.BlockSpec((1, HEAD_W), lambda b, h, i: (0, 0)),
                  pl.BlockSpec((tq, HEAD_W), lambda b, h, i: (b * nq + i, h)),
                  kv_spec, kv_spec, pl.BlockSpec(memory_space=pl.ANY)],
        out_specs=pl.BlockSpec((tq, HEAD_W), lambda b, h, i: (b * nq + i, h)),
        input_output_aliases={5: 0},
        scratch_shapes=[pltpu.VMEM((nq, tq, HEAD_W), BF16), pltpu.VMEM((nq, HEAD_W, tq), BF16)] + _attn_scratch(tq),
        compiler_params=_params(("parallel", "parallel", "arbitrary")),
        name="attn_prompt",
    )(lam_p, subln, q, k3, v3, mix)


def _softmax_rows_update(qh, kbs, vb, m_ref, l_ref, acc_ref, slot):
    for br in range(2):
        s = lax.dot_general(qh[:, br * D_HEAD:(br + 1) * D_HEAD], kbs[br],
                            (((1,), (1,)), ((), ())), preferred_element_type=F32)
        m_prev = m_ref[slot + br]
        m_new = jnp.maximum(m_prev, jnp.max(s, axis=-1, keepdims=True))
        alpha = jnp.exp(m_prev - m_new)
        p = jnp.exp(s - m_new)
        l_ref[slot + br] = alpha * l_ref[slot + br] + jnp.sum(p, axis=-1, keepdims=True)
        acc_ref[slot + br] = alpha * acc_ref[slot + br] + jnp.dot(p.astype(BF16), vb, preferred_element_type=F32)
        m_ref[slot + br] = m_new


def _attn_sample_kernel(lam_ref, subln_ref, q_ref, kn_ref, vn_ref, k1_ref, k2_ref, v1_ref, v2_ref, mix_in_ref,
                        mix_ref, m_ref, l_ref, acc_ref, *, tp, h_att, nchunks, lam_init):
    del mix_in_ref
    c = pl.program_id(1)

    @pl.when(c == 0)
    def _():
        m_ref[...] = jnp.full(m_ref.shape, NEG, F32)
        l_ref[...] = jnp.zeros(l_ref.shape, F32)
        acc_ref[...] = jnp.zeros(acc_ref.shape, F32)

    def head_rows(ref, h):
        return ref[pl.ds(h, tp, stride=h_att), :].astype(BF16)

    for h in range(h_att):
        vb = jnp.concatenate([head_rows(v1_ref, h), head_rows(v2_ref, h)], axis=1)
        _softmax_rows_update(q_ref[:, h * HEAD_W:(h + 1) * HEAD_W], (head_rows(k1_ref, h), head_rows(k2_ref, h)),
                             vb, m_ref, l_ref, acc_ref, 2 * h)

    @pl.when(c == nchunks - 1)
    def _():
        lp = lam_ref[...]
        lam = (jnp.exp(jnp.sum(lp[0:1] * lp[1:2], axis=-1, keepdims=True))
               - jnp.exp(jnp.sum(lp[2:3] * lp[3:4], axis=-1, keepdims=True)) + lam_init)
        for h in range(h_att):
            cols = slice(h * HEAD_W, (h + 1) * HEAD_W)
            kn = kn_ref[:, cols].astype(BF16)
            _softmax_rows_update(q_ref[:, cols], (kn[:, :D_HEAD], kn[:, D_HEAD:]), vn_ref[:, cols].astype(BF16),
                                 m_ref, l_ref, acc_ref, 2 * h)
            o = acc_ref[2 * h] / l_ref[2 * h] - lam * (acc_ref[2 * h + 1] / l_ref[2 * h + 1])
            o = o * lax.rsqrt(jnp.mean(o * o, axis=-1, keepdims=True) + RMS_EPS)
            mix_ref[:, cols] = (o * subln_ref[...] * (1.0 - lam_init)).astype(mix_ref.dtype)


def _attn_sample(q, k3, v3, cache_k, cache_v, layer, lam_p, subln, mix, *, row0, batch, seq, lam_init):
    h_att = cache_k.shape[3]
    d_att = h_att * HEAD_W
    past = cache_k.shape[2]
    assert past % CHUNK == 0 and seq <= CHUNK and row0 % seq == 0
    rb0 = row0 // seq
    tp = _pick(past, 512, LANES)
    nchunks = past // tp
    new_spec = pl.BlockSpec((None, seq, d_att), lambda b, c: (layer, b, 0))
    cache_k = cache_k.reshape(cache_k.shape[0], batch, past * h_att, HEAD_W)
    cache_v = cache_v.reshape(cache_v.shape[0], batch, past * h_att, HEAD_W)
    past_lo = pl.BlockSpec((None, None, tp * h_att, D_HEAD), lambda b, c: (layer, b, c, 0))
    past_hi = pl.BlockSpec((None, None, tp * h_att, D_HEAD), lambda b, c: (layer, b, c, 1))
    return pl.pallas_call(
        functools.partial(_attn_sample_kernel, tp=tp, h_att=h_att, nchunks=nchunks, lam_init=lam_init),
        out_shape=jax.ShapeDtypeStruct(mix.shape, mix.dtype),
        grid=(batch, nchunks),
        in_specs=[pl.BlockSpec((4, D_HEAD), lambda b, c: (0, 0)),
                  pl.BlockSpec((1, HEAD_W), lambda b, c: (0, 0)),
                  pl.BlockSpec((seq, d_att), lambda b, c: (rb0 + b, 0)),
                  new_spec, new_spec, past_lo, past_hi, past_lo, past_hi,
                  pl.BlockSpec(memory_space=pl.ANY)],
        out_specs=pl.BlockSpec((seq, d_att), lambda b, c: (rb0 + b, 0)),
        input_output_aliases={9: 0},
        scratch_shapes=[pltpu.VMEM((2 * h_att, seq, 1), F32), pltpu.VMEM((2 * h_att, seq, 1), F32),
                        pltpu.VMEM((2 * h_att, seq, HEAD_W), F32)],
        compiler_params=_params(("parallel", "arbitrary")),
        name="attn_sample",
    )(lam_p, subln, q, k3, v3, cache_k, cache_k, cache_v, cache_v, mix)


def _dot_f32(a, b):
    return jnp.dot(a, b, precision=lax.Precision.HIGHEST, preferred_element_type=F32)


def _lane_group_sum(x):
    x = x + pltpu.roll(x, 2 * (LANES // 4), 1)
    return x + pltpu.roll(x, LANES // 4, 1)


def _head_sum(x):
    acc = x[:, 0:LANES]
    for j in range(1, x.shape[1] // LANES):
        acc = acc + x[:, j * LANES:(j + 1) * LANES]
    return _lane_group_sum(acc)


def _tile_lanes(slab, width):
    return jnp.concatenate([slab] * (width // LANES), axis=1)


def _rwkv_prep_kernel(ur_ref, uk_ref, uv_ref, ul_ref, halo_ref, mu_ref, w0_ref, wup_ref, a0_ref, aup_ref,
                      gup_ref, kkw_ref, kaw_ref, rkw_ref,
                      r_out, w_out, k_out, vs_out, kk_out, b_out, bonus_out, g_out, *, tt, d_rwkv):
    nseg = tt // SEG
    first_row = lax.broadcasted_iota(jnp.int32, (SEG, 1), 0) == 0

    def shifted(u_ref, col0, width):
        parts = []
        for s in range(nseg):
            u = u_ref[s * SEG:(s + 1) * SEG, :]
            prev = jnp.where(first_row, halo_ref[s:s + 1, col0:col0 + width], pltpu.roll(u, 1, 0))
            parts.append(u + (prev - u) * mu_ref[:, col0:col0 + width])
        return jnp.concatenate(parts, axis=0) if nseg > 1 else parts[0]

    r = shifted(ur_ref, 0, d_rwkv)
    k = shifted(uk_ref, d_rwkv, d_rwkv)
    v = shifted(uv_ref, 2 * d_rwkv, d_rwkv)
    lo = shifted(ul_ref, 3 * d_rwkv, LORA_PAD)
    wd = lo[:, LORA_W_OFF:LORA_A_OFF]
    ad = lo[:, LORA_A_OFF:LORA_G_OFF]
    gd = lo[:, LORA_G_OFF:LORA_PAD]

    w = w0_ref[...] + _dot_f32(jnp.tanh(wd), wup_ref[...])
    x = -w
    w = -(jnp.maximum(x, 0.0) + jnp.log1p(jnp.exp(-jnp.abs(x)))) - 0.5
    decay = jnp.exp(-jnp.exp(w))
    a = jax.nn.sigmoid(a0_ref[...] + _dot_f32(ad, aup_ref[...]))
    g = _dot_f32(jax.nn.sigmoid(gd), gup_ref[...])

    kk = k * kkw_ref[...]
    inv = 1.0 / jnp.maximum(jnp.sqrt(_head_sum(kk * kk)), 1e-12)
    kk = kk * _tile_lanes(inv, d_rwkv)
    kmod = k * (1.0 + (a - 1.0) * kaw_ref[...])
    coef = _head_sum(r * kmod * rkw_ref[...])

    r_out[...] = r
    w_out[...] = decay
    k_out[...] = kmod
    kk_out[...] = kk
    b_out[...] = kk * a
    bonus_out[...] = _tile_lanes(coef, d_rwkv) * v
    g_out[...] = g

    group = lax.broadcasted_iota(jnp.int32, (tt, LANES), 1) // (LANES // 4)
    for j in range(d_rwkv // LANES):
        slab = v[:, j * LANES:(j + 1) * LANES]
        for gi in range(4):
            vs_out[pl.ds(4 * j + gi, tt, stride=N_RWKV), :] = _lane_group_sum(jnp.where(group == gi, slab, 0.0))


def _rwkv_prep(rkv, lora, halo, mu, w0, wup, a0, aup, gup, kkw, kaw, rkw):
    n = rkv.shape[0]
    d_rwkv = rkv.shape[1] // 3
    tt = PREP_ROWS
    assert n % tt == 0 and d_rwkv // N_RWKV == LANES // 4
    row = lambda width: pl.BlockSpec((1, width), lambda i: (0, 0))
    full = lambda a: pl.BlockSpec(a.shape, lambda i: (0, 0))
    tile = pl.BlockSpec((tt, d_rwkv), lambda i: (i, 0))
    nat = jax.ShapeDtypeStruct((n, d_rwkv), F32)
    return pl.pallas_call(
        functools.partial(_rwkv_prep_kernel, tt=tt, d_rwkv=d_rwkv),
        out_shape=[nat, nat, nat, jax.ShapeDtypeStruct((n * N_RWKV, LANES), F32), nat, nat, nat, nat],
        grid=(n // tt,),
        in_specs=[pl.BlockSpec((tt, d_rwkv), lambda i: (i, 0)),
                  pl.BlockSpec((tt, d_rwkv), lambda i: (i, 1)),
                  pl.BlockSpec((tt, d_rwkv), lambda i: (i, 2)),
                  pl.BlockSpec((tt, LORA_PAD), lambda i: (i, 0)),
                  pl.BlockSpec((None, tt // SEG, halo.shape[-1]), lambda i: (i, 0, 0)),
                  row(mu.shape[-1]), row(d_rwkv), full(wup), row(d_rwkv), full(aup), full(gup),
                  row(d_rwkv), row(d_rwkv), row(d_rwkv)],
        out_specs=[tile, tile, tile, pl.BlockSpec((tt * N_RWKV, LANES), lambda i: (i, 0)), tile, tile, tile, tile],
        compiler_params=_params(("parallel",)),
        name="rwkv_prep",
    )(rkv, rkv, rkv, lora, halo, mu, w0, wup, a0, aup, gup, kkw, kaw, rkw)


SCAN_SEQS = 2
SCAN_VROWS = 64
SCAN_PARTIALS = 4


def _rwkv_scan_kernel(*refs, tc, nchunks):
    nops = 5
    op_refs = [refs[e * nops:(e + 1) * nops] for e in range(SCAN_SEQS)]
    v_refs = refs[SCAN_SEQS * nops:SCAN_SEQS * (nops + 1)]
    s0_ref, o_ref, sout_ref, s_ref = refs[SCAN_SEQS * (nops + 1):]
    c = pl.program_id(1)
    nslab = s_ref.shape[1]

    @pl.when(c == 0)
    def _():
        s_ref[...] = s0_ref[...]

    vparts = N_RWKV // SCAN_VROWS

    def token(t8, i, e, p):
        base = pl.multiple_of(t8 * SUBLANES, SUBLANES)
        r_ref, w_ref, k_ref, kk_ref, b_ref = (ref.at[pl.ds(base, SUBLANES), :] for ref in op_refs[e])
        vrows = slice(p * SCAN_VROWS, (p + 1) * SCAN_VROWS)

        def row(ref, j):
            return ref[i:i + 1, j * LANES:(j + 1) * LANES]

        vt = v_refs[e][base + i, vrows, :]
        sa = [None] * SCAN_PARTIALS
        for j in range(nslab):
            term = s_ref[e, j, vrows, :] * row(kk_ref, j)
            sa[j % SCAN_PARTIALS] = term if sa[j % SCAN_PARTIALS] is None else sa[j % SCAN_PARTIALS] + term
        sa = _lane_group_sum(functools.reduce(lambda a, c: a + c, sa))
        oo = [None] * SCAN_PARTIALS
        for j in range(nslab):
            s_new = s_ref[e, j, vrows, :] * row(w_ref, j) - sa * row(b_ref, j) + vt * row(k_ref, j)
            s_ref[e, j, vrows, :] = s_new
            term = s_new * row(r_ref, j)
            oo[j % SCAN_PARTIALS] = term if oo[j % SCAN_PARTIALS] is None else oo[j % SCAN_PARTIALS] + term
        o_ref[e, base + i, vrows, :] = _lane_group_sum(functools.reduce(lambda a, c: a + c, oo))

    def step8(t8, carry):
        for i in range(SUBLANES):
            for e in range(SCAN_SEQS):
                for p in range(vparts):
                    token(t8, i, e, p)
        return carry

    lax.fori_loop(0, tc // SUBLANES, step8, 0)

    @pl.when(c == nchunks - 1)
    def _():
        sout_ref[...] = s_ref[...]


def _rwkv_scan(r, w, k, kk, b, v_scan, s0, *, row0, batch, seq):
    d_rwkv = r.shape[1]
    nslab = d_rwkv // LANES
    tc = _pick(seq, 32, SUBLANES)
    nchunks = seq // tc
    assert batch % SCAN_SEQS == 0 and row0 % tc == 0
    op_specs, v_specs = [], []
    for e in range(SCAN_SEQS):
        first = lambda g, c, e=e: (row0 + (SCAN_SEQS * g + e) * seq) // tc + c
        op_specs += [pl.BlockSpec((tc, d_rwkv), lambda g, c, f=first: (f(g, c), 0))] * 5
        v_specs.append(pl.BlockSpec((tc, N_RWKV, LANES), lambda g, c, f=first: (f(g, c), 0, 0)))
    state_spec = pl.BlockSpec((SCAN_SEQS, nslab, N_RWKV, LANES), lambda g, c: (g, 0, 0, 0))
    return pl.pallas_call(
        functools.partial(_rwkv_scan_kernel, tc=tc, nchunks=nchunks),
        out_shape=[jax.ShapeDtypeStruct((batch, seq, N_RWKV, LANES), F32),
                   jax.ShapeDtypeStruct((batch, nslab, N_RWKV, LANES), F32)],
        grid=(batch // SCAN_SEQS, nchunks),
        in_specs=op_specs + v_specs + [state_spec],
        out_specs=[pl.BlockSpec((SCAN_SEQS, tc, N_RWKV, LANES), lambda g, c: (g, c, 0, 0)), state_spec],
        scratch_shapes=[pltpu.VMEM((SCAN_SEQS, nslab, N_RWKV, LANES), F32)],
        compiler_params=_params(("parallel", "arbitrary")),
        name="rwkv_scan",
    )(*([r, w, k, kk, b] * SCAN_SEQS), *([v_scan] * SCAN_SEQS), s0)


def _rwkv_post_kernel(o_ref, bonus_ref, g_ref, lw_ref, lb_ref, mix_in_ref, out_ref, *, tt):
    del mix_in_ref
    d = bonus_ref.shape[1]
    group = lax.broadcasted_iota(jnp.int32, (tt, LANES), 1) // (LANES // 4)
    slabs = []
    for j in range(d // LANES):
        ch = [o_ref[pl.ds(4 * j + gi, tt, stride=N_RWKV), :] for gi in range(4)]
        slabs.append(jnp.where(group == 0, ch[0], jnp.where(group == 1, ch[1], jnp.where(group == 2, ch[2], ch[3]))))
    o = jnp.concatenate(slabs, axis=1)
    inv_n = 1.0 / N_RWKV
    dev = o - _tile_lanes(_head_sum(o) * inv_n, d)
    rstd = lax.rsqrt(_head_sum(dev * dev) * inv_n + LN_X_EPS)
    y = dev * _tile_lanes(rstd, d) * lw_ref[...] + lb_ref[...]
    out_ref[...] = ((y + bonus_ref[...]) * g_ref[...]).astype(out_ref.dtype)


def _rwkv_post(o_scan, bonus, g, lnx_w, lnx_b, mix, *, row0):
    rows = o_scan.shape[0] // N_RWKV
    d = bonus.shape[1]
    tt = _pick(math.gcd(rows, row0) or rows, 256, SUBLANES)
    rb = row0 // tt
    tile = pl.BlockSpec((tt, d), lambda i: (rb + i, 0))
    row = pl.BlockSpec((1, d), lambda i: (0, 0))
    return pl.pallas_call(
        functools.partial(_rwkv_post_kernel, tt=tt),
        out_shape=jax.ShapeDtypeStruct(mix.shape, mix.dtype),
        grid=(rows // tt,),
        in_specs=[pl.BlockSpec((tt * N_RWKV, LANES), lambda i: (i, 0)), tile, tile, row, row,
                  pl.BlockSpec(memory_space=pl.ANY)],
        out_specs=pl.BlockSpec((tt, d), lambda i: (rb + i, 1)),
        input_output_aliases={5: 0},
        compiler_params=_params(("parallel",)),
        name="rwkv_post",
    )(o_scan, bonus, g, lnx_w, lnx_b, mix)


def _channel_major(a, axis=-1):
    axis = axis % a.ndim
    h = a.shape[axis] // N_RWKV
    y = a.reshape(a.shape[:axis] + (h, N_RWKV) + a.shape[axis + 1:])
    return jnp.swapaxes(y, axis, axis + 1).reshape(a.shape)


def _head_major(a, axis=-1):
    axis = axis % a.ndim
    h = a.shape[axis] // N_RWKV
    y = a.reshape(a.shape[:axis] + (N_RWKV, h) + a.shape[axis + 1:])
    return jnp.swapaxes(y, axis, axis + 1).reshape(a.shape)


def _state_to_scan(s):
    b, h, nv, nk = s.shape
    return s.reshape(b, h, nv, nk // 4, 4).transpose(0, 3, 2, 4, 1).reshape(b, nk // 4, nv, 4 * h)


def _state_from_scan(s):
    b, nslab, nv, lanes = s.shape
    h = lanes // 4
    return s.reshape(b, nslab, nv, 4, h).transpose(0, 4, 2, 1, 3).reshape(b, h, nv, nslab * 4)


def _rkv_channel_major(a, d_rwkv):
    parts = [_channel_major(a[..., i * d_rwkv:(i + 1) * d_rwkv]) for i in range(3)]
    return jnp.concatenate(parts + [a[..., 3 * d_rwkv:]], axis=-1)


def _rkv_head_major(a, d_rwkv):
    parts = [_head_major(a[..., i * d_rwkv:(i + 1) * d_rwkv]) for i in range(3)]
    return jnp.concatenate(parts + [a[..., 3 * d_rwkv:]], axis=-1)


def _pad_rwkv_cols(a, d_rwkv):
    z = jnp.zeros(a.shape[:-1] + (LORA_A_OFF - DECAY_LORA,), a.dtype)
    o = 3 * d_rwkv
    return jnp.concatenate([a[..., :o + DECAY_LORA], z,
                            a[..., o + DECAY_LORA:o + DECAY_LORA + AAA_LORA], z,
                            a[..., o + DECAY_LORA + AAA_LORA:]], axis=-1)


def _unpad_rwkv_cols(a, d_rwkv):
    o = 3 * d_rwkv
    return jnp.concatenate([a[..., :o + DECAY_LORA],
                            a[..., o + LORA_A_OFF:o + LORA_A_OFF + AAA_LORA],
                            a[..., o + LORA_G_OFF:]], axis=-1)


def _pad_rows(w, rows):
    return jnp.concatenate([w, jnp.zeros((rows - w.shape[0], w.shape[1]), w.dtype)], axis=0)


def kernel(x_prompt, x_sample, cache_k, cache_v, state_shift, state_wkv, norm_mix, w_in, lam_q1, lam_k1, lam_q2, lam_k2, attn_subln, shift_mu, decay_w0, decay_up, iclr_a0, iclr_up, gate_up, key_kk, key_ka, bonus_rk, lnx_w, lnx_b, w_out, norm_ffn, ffn_up, ffn_down, norm_final):
    bp, tp, d_model = x_prompt.shape
    bs, ts, _ = x_sample.shape
    depth = w_in.shape[0]
    d_att = d_model // 2
    h_att = d_att // HEAD_W
    d_rwkv = d_model - d_att
    h_rwkv = d_rwkv // N_RWKV
    att_cols = 3 * d_att
    n_p, n_s = bp * tp, bs * ts
    n = n_p + n_s
    past = cache_k.shape[2]
    assert tp % SEG == 0 and ts % SEG == 0 and SEG == CHUNK

    x = jnp.concatenate([x_prompt.reshape(n_p, d_model), x_sample.reshape(n_s, d_model)], axis=0)
    rkv_cols = att_cols + 3 * d_rwkv
    w_lora = _pad_rwkv_cols(w_in[:, :, rkv_cols:], 0)
    w_rkv = _rkv_channel_major(w_in[:, :, att_cols:rkv_cols], d_rwkv)
    w_out_cm = jnp.concatenate([w_out[:, :d_att], _channel_major(w_out[:, d_att:], axis=1)], axis=1)
    row_cm = lambda a: _channel_major(a.reshape(-1)).reshape(1, -1)

    seg_p = jnp.arange(n_p // SEG) % (tp // SEG) == 0
    seg_s = jnp.arange(n_s // SEG) % (ts // SEG) == 0

    outs = {name: [] for name in ("sp", "wp", "ss", "ws")}
    kp = jnp.zeros((depth, n_p, d_att), F32)
    vp = jnp.zeros((depth, n_p, d_att), F32)
    ks = jnp.zeros((depth, n_s, d_att), F32)
    vs = jnp.zeros((depth, n_s, d_att), F32)
    mix = jnp.zeros((n, d_model), BF16)
    for l in range(depth):
        lam_init = 0.8 - 0.6 * math.exp(-0.3 * l)
        xn = _rmsnorm(x, norm_mix[l], BF16)
        q = _matmul(xn, w_in, l, col0=0, ncols=d_att, scale=D_HEAD ** -0.5, out_dtype=BF16, name="w_in_q")
        kp = _matmul_stacked(xn, w_in, l, kp, col0=d_att, row0=0, name="w_in_k")
        vp = _matmul_stacked(xn, w_in, l, vp, col0=2 * d_att, row0=0, name="w_in_v")
        ks = _matmul_stacked(xn, w_in, l, ks, col0=d_att, row0=n_p, name="w_in_ks")
        vs = _matmul_stacked(xn, w_in, l, vs, col0=2 * d_att, row0=n_p, name="w_in_vs")
        rkv = _matmul(xn, w_rkv, l, name="w_in_rkv")
        lora = _matmul(xn, w_lora, l, name="w_in_lora")

        lam_p = jnp.stack([lam_q1[l], lam_k1[l], lam_q2[l], lam_k2[l]])
        subln = attn_subln[l].reshape(1, HEAD_W)
        mix = _attn_prompt(q, kp, vp, l, lam_p, subln, mix, batch=bp, seq=tp, lam_init=lam_init)
        mix = _attn_sample(q, ks, vs, cache_k, cache_v, l, lam_p, subln, mix, row0=n_p, batch=bs, seq=ts,
                           lam_init=lam_init)

        u_last = jnp.concatenate([rkv[SEG - 1::SEG], lora[SEG - 1::SEG]], axis=1)
        prev = jnp.concatenate([jnp.zeros_like(u_last[:1]), u_last[:-1]], axis=0)
        st_p = jnp.zeros((n_p // SEG, u_last.shape[-1]), F32)
        shift_in = _rkv_channel_major(_pad_rwkv_cols(state_shift[l][:, 0, :], d_rwkv), d_rwkv)
        st_s = jnp.repeat(shift_in, ts // SEG, axis=0)
        start = jnp.concatenate([seg_p, seg_s])[:, None]
        halo = jnp.where(start, jnp.concatenate([st_p, st_s], axis=0), prev)
        halo = halo.reshape(n // PREP_ROWS, PREP_ROWS // SEG, halo.shape[-1])

        mu = _rkv_channel_major(_pad_rwkv_cols(shift_mu[l], d_rwkv), d_rwkv).reshape(1, -1)
        r, w, k, v_scan, kk, b, bonus, g = _rwkv_prep(
            rkv, lora, halo, mu,
            row_cm(decay_w0[l]), _pad_rows(_channel_major(decay_up[l]), LORA_A_OFF - LORA_W_OFF),
            row_cm(iclr_a0[l]), _pad_rows(_channel_major(iclr_up[l]), LORA_G_OFF - LORA_A_OFF),
            _channel_major(gate_up[l]), row_cm(key_kk[l]), row_cm(key_ka[l]), row_cm(bonus_rk[l]))
        v_scan = v_scan.reshape(n, N_RWKV, LANES)

        scan_ops = (r, w, k, kk, b, v_scan)
        s0_p = jnp.zeros((bp, d_rwkv // LANES, N_RWKV, LANES), F32)
        o_p, s_p = _rwkv_scan(*scan_ops, s0_p, row0=0, batch=bp, seq=tp)
        o_s, s_s = _rwkv_scan(*scan_ops, _state_to_scan(state_wkv[l]), row0=n_p, batch=bs, seq=ts)
        post_w = (row_cm(lnx_w[l]), row_cm(lnx_b[l]))
        mix = _rwkv_post(o_p.reshape(n_p * N_RWKV, LANES), bonus, g, *post_w, mix, row0=0)
        mix = _rwkv_post(o_s.reshape(n_s * N_RWKV, LANES), bonus, g, *post_w, mix, row0=n_p)

        x = _matmul(mix, w_out_cm, l, residual=x, name="w_out")
        hn = _rmsnorm(x, norm_ffn[l], BF16)
        hid = _matmul(hn, ffn_up, l, act="relu2", out_dtype=BF16, name="ffn_up")
        x = _matmul(hid, ffn_down, l, residual=x, name="ffn_down")

        u_fin = _rkv_head_major(u_last, d_rwkv).reshape(-1, 1, u_last.shape[-1])
        outs["sp"].append(_unpad_rwkv_cols(u_fin[tp // SEG - 1:n_p // SEG:tp // SEG], d_rwkv))
        outs["wp"].append(_state_from_scan(s_p))
        outs["ss"].append(_unpad_rwkv_cols(u_fin[n_p // SEG + ts // SEG - 1::ts // SEG], d_rwkv))
        outs["ws"].append(_state_from_scan(s_s))

    y_p = _rmsnorm(x, norm_final, F32, row0=0, rows=n_p)
    y_s = _rmsnorm(x, norm_final, F32, row0=n_p, rows=n_s)
    return (y_p.reshape(bp, tp, d_model), y_s.reshape(bs, ts, d_model),
            kp.reshape(depth, bp, tp, h_att, HEAD_W), vp.reshape(depth, bp, tp, h_att, HEAD_W),
            jnp.stack(outs["sp"]), jnp.stack(outs["wp"]),
            ks.reshape(depth, bs, ts, h_att, HEAD_W), vs.reshape(depth, bs, ts, h_att, HEAD_W),
            jnp.stack(outs["ss"]), jnp.stack(outs["ws"]))
```

```python
import functools
import math

import jax
import jax.numpy as jnp
from jax import lax
from jax.experimental import pallas as pl
from jax.experimental.pallas import tpu as pltpu

F32 = jnp.float32
BF16 = jnp.bfloat16

D_HEAD = 128
HEAD_W = 2 * D_HEAD
CHUNK = 64
N_RWKV = 64
DECAY_LORA = 96
AAA_LORA = 96
GATE_LORA = 256
RMS_EPS = 1e-5
LN_X_EPS = 64e-5
NEG = -1e30

LANES = 128
SUBLANES = 8
VMEM_LIMIT = 56 * 1024 * 1024

LORA_W_OFF, LORA_A_OFF, LORA_G_OFF, LORA_PAD = 0, 128, 256, 512
SEG = 64
PREP_ROWS = 128
SCAN_CHUNK = 32


def _pick(n, pref, mult):
    best = None
    for d in range(mult, min(n, pref) + 1, mult):
        if n % d == 0:
            best = d
    assert best is not None, (n, pref, mult)
    return best


def _params(sem, flags=None):
    return pltpu.CompilerParams(dimension_semantics=sem, vmem_limit_bytes=VMEM_LIMIT, flags=flags)


def _rmsnorm_kernel(x_ref, g_ref, o_ref):
    x = x_ref[...]
    ms = jnp.mean(x * x, axis=-1, keepdims=True)
    o_ref[...] = (x * lax.rsqrt(ms + RMS_EPS) * g_ref[...]).astype(o_ref.dtype)


def _rmsnorm(x, g, out_dtype, row0=0, rows=None):
    n, d = x.shape
    rows = n - row0 if rows is None else rows
    tr = _pick(math.gcd(rows, row0) or rows, 256, SUBLANES)
    rb = row0 // tr
    return pl.pallas_call(
        _rmsnorm_kernel,
        out_shape=jax.ShapeDtypeStruct((rows, d), out_dtype),
        grid=(rows // tr,),
        in_specs=[pl.BlockSpec((tr, d), lambda i: (rb + i, 0)),
                  pl.BlockSpec((1, d), lambda i: (0, 0))],
        out_specs=pl.BlockSpec((tr, d), lambda i: (i, 0)),
        compiler_params=_params(("parallel",)),
        name="rmsnorm",
    )(x, g.reshape(1, d))


def _mm_kernel(*refs, nk, has_res, has_alias, act, scale):
    refs = list(refs)
    x_ref, w_ref = refs[0], refs[1]
    r_ref = refs[2] if has_res else None
    o_ref = refs[2 + has_res + has_alias]
    acc_ref = refs[-1] if nk > 1 else None

    def epilogue(y):
        if act == "relu2":
            y = jnp.square(jnp.maximum(y, 0.0))
        if scale is not None:
            y = y * scale
        if has_res:
            y = y + r_ref[...]
        o_ref[...] = y.astype(o_ref.dtype).reshape(o_ref.shape)

    if nk == 1:
        epilogue(jnp.dot(x_ref[...], w_ref[...].astype(BF16), preferred_element_type=F32))
        return
    k = pl.program_id(2)

    @pl.when(k == 0)
    def _():
        acc_ref[...] = jnp.zeros_like(acc_ref)

    acc_ref[...] += jnp.dot(x_ref[...], w_ref[...].astype(BF16), preferred_element_type=F32)

    @pl.when(k == nk - 1)
    def _():
        epilogue(acc_ref[...])


FULL_K = 4096


def _mm_tiles(kd):
    return (1024, 512, kd) if kd <= FULL_K else (1024, 1024, 2048)


def _matmul(x, w3, layer, *, col0=0, ncols=None, residual=None, act=None, scale=None, out_dtype=F32,
            name="matmul"):
    m, kd = x.shape
    ncols = w3.shape[2] - col0 if ncols is None else ncols
    tm, tn, tk = _mm_tiles(kd)
    tm, tn, tk = _pick(m, tm, SUBLANES), _pick(math.gcd(ncols, col0) or ncols, tn, LANES), _pick(kd, tk, LANES)
    nk = kd // tk
    cb = col0 // tn
    in_specs = [pl.BlockSpec((tm, tk), lambda i, j, k: (i, k)),
                pl.BlockSpec((None, tk, tn), lambda i, j, k: (layer, k, cb + j))]
    args = [x, w3]
    if residual is not None:
        in_specs.append(pl.BlockSpec((tm, tn), lambda i, j, k: (i, j)))
        args.append(residual)
    return pl.pallas_call(
        functools.partial(_mm_kernel, nk=nk, has_res=residual is not None, has_alias=False, act=act, scale=scale),
        out_shape=jax.ShapeDtypeStruct((m, ncols), out_dtype),
        grid=(m // tm, ncols // tn, nk),
        in_specs=in_specs,
        out_specs=pl.BlockSpec((tm, tn), lambda i, j, k: (i, j)),
        scratch_shapes=[pltpu.VMEM((tm, tn), F32)] if nk > 1 else [],
        compiler_params=_params(("parallel", "parallel", "arbitrary")),
        name=name,
    )(*args)


def _matmul_stacked(x, w3, layer, stack, *, col0, row0, name="matmul_stacked"):
    m, kd = x.shape
    depth, rows, ncols = stack.shape
    tm, tn, tk = _mm_tiles(kd)
    tm, tn, tk = _pick(math.gcd(rows, row0) or rows, tm, SUBLANES), _pick(math.gcd(ncols, col0), tn, LANES), \
        _pick(kd, tk, LANES)
    nk = kd // tk
    rb, cb = row0 // tm, col0 // tn
    return pl.pallas_call(
        functools.partial(_mm_kernel, nk=nk, has_res=False, has_alias=True, act=None, scale=None),
        out_shape=jax.ShapeDtypeStruct(stack.shape, stack.dtype),
        grid=(rows // tm, ncols // tn, nk),
        in_specs=[pl.BlockSpec((tm, tk), lambda i, j, k: (rb + i, k)),
                  pl.BlockSpec((None, tk, tn), lambda i, j, k: (layer, k, cb + j)),
                  pl.BlockSpec(memory_space=pl.ANY)],
        out_specs=pl.BlockSpec((None, tm, tn), lambda i, j, k: (layer, i, j)),
        scratch_shapes=[pltpu.VMEM((tm, tn), F32)] if nk > 1 else [],
        input_output_aliases={2: 0},
        compiler_params=_params(("parallel", "parallel", "arbitrary")),
        name=name,
    )(x, w3, stack)


def _softmax_tile_update(qt_ref, kb, vt, mask, m_ref, l_ref, acc_ref):
    for br in range(2):
        km = kb[:, br * D_HEAD:(br + 1) * D_HEAD]
        st = jnp.dot(km, qt_ref[br * D_HEAD:(br + 1) * D_HEAD, :], preferred_element_type=F32)
        if mask is not None:
            st = jnp.where(mask, st, NEG)
        m_prev = m_ref[br]
        m_new = jnp.maximum(m_prev, jnp.max(st, axis=0, keepdims=True))
        alpha = jnp.exp(m_prev - m_new)
        p = jnp.exp(st - m_new)
        l_ref[br] = alpha * l_ref[br] + jnp.sum(p, axis=0, keepdims=True)
        acc_ref[br] = alpha * acc_ref[br] + jnp.dot(vt, p.astype(BF16), preferred_element_type=F32)
        m_ref[br] = m_new


def _attn_init(q_ref, qt_ref, m_ref, l_ref, acc_ref):
    qt_ref[...] = q_ref[...].astype(F32).T.astype(BF16)
    m_ref[...] = jnp.full(m_ref.shape, NEG, F32)
    l_ref[...] = jnp.zeros(l_ref.shape, F32)
    acc_ref[...] = jnp.zeros(acc_ref.shape, F32)


def _attn_finish(lam_ref, subln_ref, o_ref, l_ref, acc_ref, lam_init):
    lp = lam_ref[...]
    lam = (jnp.exp(jnp.sum(lp[0:1] * lp[1:2], axis=-1, keepdims=True))
           - jnp.exp(jnp.sum(lp[2:3] * lp[3:4], axis=-1, keepdims=True)) + lam_init)
    ot = acc_ref[0] / l_ref[0] - lam * (acc_ref[1] / l_ref[1])
    ot = ot * lax.rsqrt(jnp.mean(ot * ot, axis=0, keepdims=True) + RMS_EPS)
    o_ref[...] = (ot.T * subln_ref[...] * (1.0 - lam_init)).astype(o_ref.dtype)


def _attn_prompt_kernel(lam_ref, subln_ref, q_ref, k_ref, v_ref, mix_in_ref, mix_ref, kb_ref, vt_ref, qt_ref,
                        m_ref, l_ref, acc_ref, *, tq, lam_init):
    del mix_in_ref
    i = pl.program_id(2)
    ntiles = kb_ref.shape[0]

    @pl.when(i == 0)
    def _():
        for j in range(ntiles):
            kb_ref[j] = k_ref[j * tq:(j + 1) * tq, :].astype(BF16)
            vt_ref[j] = v_ref[j * tq:(j + 1) * tq, :].T.astype(BF16)

    _attn_init(q_ref, qt_ref, m_ref, l_ref, acc_ref)

    def body(j, carry):
        _softmax_tile_update(qt_ref, kb_ref[j], vt_ref[j], None, m_ref, l_ref, acc_ref)
        return carry

    lax.fori_loop(0, i, body, 0)
    key_chunk = lax.broadcasted_iota(jnp.int32, (tq, tq), 0) // CHUNK
    qry_chunk = lax.broadcasted_iota(jnp.int32, (tq, tq), 1) // CHUNK
    _softmax_tile_update(qt_ref, kb_ref[i], vt_ref[i], key_chunk <= qry_chunk, m_ref, l_ref, acc_ref)
    _attn_finish(lam_ref, subln_ref, mix_ref, l_ref, acc_ref, lam_init)


def _attn_scratch(tq):
    return [pltpu.VMEM((HEAD_W, tq), BF16),
            pltpu.VMEM((2, 1, tq), F32),
            pltpu.VMEM((2, 1, tq), F32),
            pltpu.VMEM((2, HEAD_W, tq), F32)]


def _attn_prompt(q, k3, v3, layer, lam_p, subln, mix, *, batch, seq, lam_init):
    h_att = k3.shape[2] // HEAD_W
    tq = _pick(seq, 512, CHUNK)
    nq = seq // tq
    kv_spec = pl.BlockSpec((None, seq, HEAD_W), lambda b, h, i: (layer, b, h))
    return pl.pallas_call(
        functools.partial(_attn_prompt_kernel, tq=tq, lam_init=lam_init),
        out_shape=jax.ShapeDtypeStruct(mix.shape, mix.dtype),
        grid=(batch, h_att, nq),
        in_specs=[pl.BlockSpec((4, D_HEAD), lambda b, h, i: (0, 0)),
                  pl.BlockSpec((1, HEAD_W), lambda b, h, i: (0, 0)),
                  pl.BlockSpec((tq, HEAD_W), lambda b, h, i: (b * nq + i, h)),
                  kv_spec, kv_spec, pl.BlockSpec(memory_space=pl.ANY)],
        out_specs=pl.BlockSpec((tq, HEAD_W), lambda b, h, i: (b * nq + i, h)),
        input_output_aliases={5: 0},
        scratch_shapes=[pltpu.VMEM((nq, tq, HEAD_W), BF16), pltpu.VMEM((nq, HEAD_W, tq), BF16)] + _attn_scratch(tq),
        compiler_params=_params(("parallel", "parallel", "arbitrary")),
        name="attn_prompt",
    )(lam_p, subln, q, k3, v3, mix)


def _softmax_rows_update(qh, kbs, vb, m_ref, l_ref, acc_ref, slot):
    for br in range(2):
        s = lax.dot_general(qh[:, br * D_HEAD:(br + 1) * D_HEAD], kbs[br],
                            (((1,), (1,)), ((), ())), preferred_element_type=F32)
        m_prev = m_ref[slot + br]
        m_new = jnp.maximum(m_prev, jnp.max(s, axis=-1, keepdims=True))
        alpha = jnp.exp(m_prev - m_new)
        p = jnp.exp(s - m_new)
        l_ref[slot + br] = alpha * l_ref[slot + br] + jnp.sum(p, axis=-1, keepdims=True)
        acc_ref[slot + br] = alpha * acc_ref[slot + br] + jnp.dot(p.astype(BF16), vb, preferred_element_type=F32)
        m_ref[slot + br] = m_new


def _attn_sample_kernel(lam_ref, subln_ref, q_ref, kn_ref, vn_ref, k1_ref, k2_ref, v1_ref, v2_ref, mix_in_ref,
                        mix_ref, m_ref, l_ref, acc_ref, *, tp, h_att, nchunks, lam_init):
    del mix_in_ref
    c = pl.program_id(1)

    @pl.when(c == 0)
    def _():
        m_ref[...] = jnp.full(m_ref.shape, NEG, F32)
        l_ref[...] = jnp.zeros(l_ref.shape, F32)
        acc_ref[...] = jnp.zeros(acc_ref.shape, F32)

    def head_rows(ref, h):
        return ref[pl.ds(h, tp, stride=h_att), :].astype(BF16)

    for h in range(h_att):
        vb = jnp.concatenate([head_rows(v1_ref, h), head_rows(v2_ref, h)], axis=1)
        _softmax_rows_update(q_ref[:, h * HEAD_W:(h + 1) * HEAD_W], (head_rows(k1_ref, h), head_rows(k2_ref, h)),
                             vb, m_ref, l_ref, acc_ref, 2 * h)

    @pl.when(c == nchunks - 1)
    def _():
        lp = lam_ref[...]
        lam = (jnp.exp(jnp.sum(lp[0:1] * lp[1:2], axis=-1, keepdims=True))
               - jnp.exp(jnp.sum(lp[2:3] * lp[3:4], axis=-1, keepdims=True)) + lam_init)
        for h in range(h_att):
            cols = slice(h * HEAD_W, (h + 1) * HEAD_W)
            kn = kn_ref[:, cols].astype(BF16)
            _softmax_rows_update(q_ref[:, cols], (kn[:, :D_HEAD], kn[:, D_HEAD:]), vn_ref[:, cols].astype(BF16),
                                 m_ref, l_ref, acc_ref, 2 * h)
            o = acc_ref[2 * h] / l_ref[2 * h] - lam * (acc_ref[2 * h + 1] / l_ref[2 * h + 1])
            o = o * lax.rsqrt(jnp.mean(o * o, axis=-1, keepdims=True) + RMS_EPS)
            mix_ref[:, cols] = (o * subln_ref[...] * (1.0 - lam_init)).astype(mix_ref.dtype)


def _attn_sample(q, k3, v3, cache_k, cache_v, layer, lam_p, subln, mix, *, row0, batch, seq, lam_init):
    h_att = cache_k.shape[3]
    d_att = h_att * HEAD_W
    past = cache_k.shape[2]
    assert past % CHUNK == 0 and seq <= CHUNK and row0 % seq == 0
    rb0 = row0 // seq
    tp = _pick(past, 512, LANES)
    nchunks = past // tp
    new_spec = pl.BlockSpec((None, seq, d_att), lambda b, c: (layer, b, 0))
    cache_k = cache_k.reshape(cache_k.shape[0], batch, past * h_att, HEAD_W)
    cache_v = cache_v.reshape(cache_v.shape[0], batch, past * h_att, HEAD_W)
    past_lo = pl.BlockSpec((None, None, tp * h_att, D_HEAD), lambda b, c: (layer, b, c, 0))
    past_hi = pl.BlockSpec((None, None, tp * h_att, D_HEAD), lambda b, c: (layer, b, c, 1))
    return pl.pallas_call(
        functools.partial(_attn_sample_kernel, tp=tp, h_att=h_att, nchunks=nchunks, lam_init=lam_init),
        out_shape=jax.ShapeDtypeStruct(mix.shape, mix.dtype),
        grid=(batch, nchunks),
        in_specs=[pl.BlockSpec((4, D_HEAD), lambda b, c: (0, 0)),
                  pl.BlockSpec((1, HEAD_W), lambda b, c: (0, 0)),
                  pl.BlockSpec((seq, d_att), lambda b, c: (rb0 + b, 0)),
                  new_spec, new_spec, past_lo, past_hi, past_lo, past_hi,
                  pl.BlockSpec(memory_space=pl.ANY)],
        out_specs=pl.BlockSpec((seq, d_att), lambda b, c: (rb0 + b, 0)),
        input_output_aliases={9: 0},
        scratch_shapes=[pltpu.VMEM((2 * h_att, seq, 1), F32), pltpu.VMEM((2 * h_att, seq, 1), F32),
                        pltpu.VMEM((2 * h_att, seq, HEAD_W), F32)],
        compiler_params=_params(("parallel", "arbitrary")),
        name="attn_sample",
    )(lam_p, subln, q, k3, v3, cache_k, cache_k, cache_v, cache_v, mix)


def _dot_f32(a, b):
    return jnp.dot(a, b, precision=lax.Precision.HIGHEST, preferred_element_type=F32)


def _lane_group_sum(x):
    x = x + pltpu.roll(x, 2 * (LANES // 4), 1)
    return x + pltpu.roll(x, LANES // 4, 1)


def _head_sum(x):
    acc = x[:, 0:LANES]
    for j in range(1, x.shape[1] // LANES):
        acc = acc + x[:, j * LANES:(j + 1) * LANES]
    return _lane_group_sum(acc)


def _tile_lanes(slab, width):
    return jnp.concatenate([slab] * (width // LANES), axis=1)


def _rwkv_prep_kernel(ur_ref, uk_ref, uv_ref, ul_ref, halo_ref, mu_ref, w0_ref, wup_ref, a0_ref, aup_ref,
                      gup_ref, kkw_ref, kaw_ref, rkw_ref,
                      r_out, w_out, k_out, vs_out, kk_out, b_out, bonus_out, g_out, *, tt, d_rwkv):
    nseg = tt // SEG
    first_row = lax.broadcasted_iota(jnp.int32, (SEG, 1), 0) == 0

    def shifted(u_ref, col0, width):
        parts = []
        for s in range(nseg):
            u = u_ref[s * SEG:(s + 1) * SEG, :]
            prev = jnp.where(first_row, halo_ref[s:s + 1, col0:col0 + width], pltpu.roll(u, 1, 0))
            parts.append(u + (prev - u) * mu_ref[:, col0:col0 + width])
        return jnp.concatenate(parts, axis=0) if nseg > 1 else parts[0]

    r = shifted(ur_ref, 0, d_rwkv)
    k = shifted(uk_ref, d_rwkv, d_rwkv)
    v = shifted(uv_ref, 2 * d_rwkv, d_rwkv)
    lo = shifted(ul_ref, 3 * d_rwkv, LORA_PAD)
    wd = lo[:, LORA_W_OFF:LORA_A_OFF]
    ad = lo[:, LORA_A_OFF:LORA_G_OFF]
    gd = lo[:, LORA_G_OFF:LORA_PAD]

    w = w0_ref[...] + _dot_f32(jnp.tanh(wd), wup_ref[...])
    x = -w
    w = -(jnp.maximum(x, 0.0) + jnp.log1p(jnp.exp(-jnp.abs(x)))) - 0.5
    rate = jnp.exp(w)
    pos = lax.broadcasted_iota(jnp.int32, (tt, 1), 0) % SCAN_CHUNK
    cum = rate
    shift = 1
    while shift < SCAN_CHUNK:
        cum = cum + jnp.where(pos >= shift, pltpu.roll(cum, shift, 0), 0.0)
        shift *= 2
    decay_to = jnp.exp(-cum)
    decay_before = jnp.exp(rate - cum)
    undo_decay = jnp.exp(cum)
    a = jax.nn.sigmoid(a0_ref[...] + _dot_f32(ad, aup_ref[...]))
    g = _dot_f32(jax.nn.sigmoid(gd), gup_ref[...])

    kk = k * kkw_ref[...]
    inv = 1.0 / jnp.maximum(jnp.sqrt(_head_sum(kk * kk)), 1e-12)
    kk = kk * _tile_lanes(inv, d_rwkv)
    kmod = k * (1.0 + (a - 1.0) * kaw_ref[...])
    coef = _head_sum(r * kmod * rkw_ref[...])

    r_out[...] = r * decay_to
    w_out[...] = decay_to
    k_out[...] = kmod * undo_decay
    kk_out[...] = kk * decay_before
    b_out[...] = kk * a * undo_decay
    bonus_out[...] = _tile_lanes(coef, d_rwkv) * v
    g_out[...] = g

    group = lax.broadcasted_iota(jnp.int32, (tt, LANES), 1) // (LANES // 4)
    for j in range(d_rwkv // LANES):
        slab = v[:, j * LANES:(j + 1) * LANES]
        for gi in range(4):
            vs_out[pl.ds(4 * j + gi, tt, stride=N_RWKV), :] = _lane_group_sum(jnp.where(group == gi, slab, 0.0))


def _rwkv_prep(rkv, lora, halo, mu, w0, wup, a0, aup, gup, kkw, kaw, rkw):
    n = rkv.shape[0]
    d_rwkv = rkv.shape[1] // 3
    tt = PREP_ROWS
    assert n % tt == 0 and d_rwkv // N_RWKV == LANES // 4
    row = lambda width: pl.BlockSpec((1, width), lambda i: (0, 0))
    full = lambda a: pl.BlockSpec(a.shape, lambda i: (0, 0))
    tile = pl.BlockSpec((tt, d_rwkv), lambda i: (i, 0))
    nat = jax.ShapeDtypeStruct((n, d_rwkv), F32)
    return pl.pallas_call(
        functools.partial(_rwkv_prep_kernel, tt=tt, d_rwkv=d_rwkv),
        out_shape=[nat, nat, nat, jax.ShapeDtypeStruct((n * N_RWKV, LANES), F32), nat, nat, nat, nat],
        grid=(n // tt,),
        in_specs=[pl.BlockSpec((tt, d_rwkv), lambda i: (i, 0)),
                  pl.BlockSpec((tt, d_rwkv), lambda i: (i, 1)),
                  pl.BlockSpec((tt, d_rwkv), lambda i: (i, 2)),
                  pl.BlockSpec((tt, LORA_PAD), lambda i: (i, 0)),
                  pl.BlockSpec((None, tt // SEG, halo.shape[-1]), lambda i: (i, 0, 0)),
                  row(mu.shape[-1]), row(d_rwkv), full(wup), row(d_rwkv), full(aup), full(gup),
                  row(d_rwkv), row(d_rwkv), row(d_rwkv)],
        out_specs=[tile, tile, tile, pl.BlockSpec((tt * N_RWKV, LANES), lambda i: (i, 0)), tile, tile, tile, tile],
        compiler_params=_params(("parallel",)),
        name="rwkv_prep",
    )(rkv, rkv, rkv, lora, halo, mu, w0, wup, a0, aup, gup, kkw, kaw, rkw)


SCAN_SEQS = 2
SCAN_VROWS = 64
SCAN_PARTIALS = 4


def _rwkv_scan_kernel(*refs, tc, nchunks):
    nops = 5
    op_refs = [refs[e * nops:(e + 1) * nops] for e in range(SCAN_SEQS)]
    v_refs = refs[SCAN_SEQS * nops:SCAN_SEQS * (nops + 1)]
    s0_ref, o_ref, sout_ref, s_ref = refs[SCAN_SEQS * (nops + 1):]
    c = pl.program_id(1)
    nslab = s_ref.shape[1]

    @pl.when(c == 0)
    def _():
        s_ref[...] = s0_ref[...]

    vparts = N_RWKV // SCAN_VROWS

    def token(t8, i, e, p):
        base = pl.multiple_of(t8 * SUBLANES, SUBLANES)
        r_ref, _, k_ref, kk_ref, b_ref = (ref.at[pl.ds(base, SUBLANES), :] for ref in op_refs[e])
        vrows = slice(p * SCAN_VROWS, (p + 1) * SCAN_VROWS)

        def row(ref, j):
            return ref[i:i + 1, j * LANES:(j + 1) * LANES]

        vt = v_refs[e][base + i, vrows, :]
        sa = [None] * SCAN_PARTIALS
        for j in range(nslab):
            term = s_ref[e, j, vrows, :] * row(kk_ref, j)
            sa[j % SCAN_PARTIALS] = term if sa[j % SCAN_PARTIALS] is None else sa[j % SCAN_PARTIALS] + term
        sa = _lane_group_sum(functools.reduce(lambda a, c: a + c, sa))
        oo = [None] * SCAN_PARTIALS
        for j in range(nslab):
            s_new = s_ref[e, j, vrows, :] + (vt * row(k_ref, j) - sa * row(b_ref, j))
            s_ref[e, j, vrows, :] = s_new
            term = s_new * row(r_ref, j)
            oo[j % SCAN_PARTIALS] = term if oo[j % SCAN_PARTIALS] is None else oo[j % SCAN_PARTIALS] + term
        o_ref[e, base + i, vrows, :] = _lane_group_sum(functools.reduce(lambda a, c: a + c, oo))

    def step8(t8, carry):
        for i in range(SUBLANES):
            for e in range(SCAN_SEQS):
                for p in range(vparts):
                    token(t8, i, e, p)
        return carry

    lax.fori_loop(0, tc // SUBLANES, step8, 0)

    for e in range(SCAN_SEQS):
        decay_ref = op_refs[e][1]
        for j in range(nslab):
            s_ref[e, j] = s_ref[e, j] * decay_ref[tc - 1:tc, j * LANES:(j + 1) * LANES]

    @pl.when(c == nchunks - 1)
    def _():
        sout_ref[...] = s_ref[...]


def _rwkv_scan(r, w, k, kk, b, v_scan, s0, *, row0, batch, seq):
    d_rwkv = r.shape[1]
    nslab = d_rwkv // LANES
    tc = SCAN_CHUNK
    nchunks = seq // tc
    assert batch % SCAN_SEQS == 0 and row0 % tc == 0 and seq % tc == 0 and PREP_ROWS % tc == 0
    op_specs, v_specs = [], []
    for e in range(SCAN_SEQS):
        first = lambda g, c, e=e: (row0 + (SCAN_SEQS * g + e) * seq) // tc + c
        op_specs += [pl.BlockSpec((tc, d_rwkv), lambda g, c, f=first: (f(g, c), 0))] * 5
        v_specs.append(pl.BlockSpec((tc, N_RWKV, LANES), lambda g, c, f=first: (f(g, c), 0, 0)))
    state_spec = pl.BlockSpec((SCAN_SEQS, nslab, N_RWKV, LANES), lambda g, c: (g, 0, 0, 0))
    return pl.pallas_call(
        functools.partial(_rwkv_scan_kernel, tc=tc, nchunks=nchunks),
        out_shape=[jax.ShapeDtypeStruct((batch, seq, N_RWKV, LANES), F32),
                   jax.ShapeDtypeStruct((batch, nslab, N_RWKV, LANES), F32)],
        grid=(batch // SCAN_SEQS, nchunks),
        in_specs=op_specs + v_specs + [state_spec],
        out_specs=[pl.BlockSpec((SCAN_SEQS, tc, N_RWKV, LANES), lambda g, c: (g, c, 0, 0)), state_spec],
        scratch_shapes=[pltpu.VMEM((SCAN_SEQS, nslab, N_RWKV, LANES), F32)],
        compiler_params=_params(("parallel", "arbitrary")),
        name="rwkv_scan",
    )(*([r, w, k, kk, b] * SCAN_SEQS), *([v_scan] * SCAN_SEQS), s0)


def _rwkv_post_kernel(o_ref, bonus_ref, g_ref, lw_ref, lb_ref, mix_in_ref, out_ref, *, tt):
    del mix_in_ref
    d = bonus_ref.shape[1]
    group = lax.broadcasted_iota(jnp.int32, (tt, LANES), 1) // (LANES // 4)
    slabs = []
    for j in range(d // LANES):
        ch = [o_ref[pl.ds(4 * j + gi, tt, stride=N_RWKV), :] for gi in range(4)]
        slabs.append(jnp.where(group == 0, ch[0], jnp.where(group == 1, ch[1], jnp.where(group == 2, ch[2], ch[3]))))
    o = jnp.concatenate(slabs, axis=1)
    inv_n = 1.0 / N_RWKV
    dev = o - _tile_lanes(_head_sum(o) * inv_n, d)
    rstd = lax.rsqrt(_head_sum(dev * dev) * inv_n + LN_X_EPS)
    y = dev * _tile_lanes(rstd, d) * lw_ref[...] + lb_ref[...]
    out_ref[...] = ((y + bonus_ref[...]) * g_ref[...]).astype(out_ref.dtype)


def _rwkv_post(o_scan, bonus, g, lnx_w, lnx_b, mix, *, row0):
    rows = o_scan.shape[0] // N_RWKV
    d = bonus.shape[1]
    tt = _pick(math.gcd(rows, row0) or rows, 256, SUBLANES)
    rb = row0 // tt
    tile = pl.BlockSpec((tt, d), lambda i: (rb + i, 0))
    row = pl.BlockSpec((1, d), lambda i: (0, 0))
    return pl.pallas_call(
        functools.partial(_rwkv_post_kernel, tt=tt),
        out_shape=jax.ShapeDtypeStruct(mix.shape, mix.dtype),
        grid=(rows // tt,),
        in_specs=[pl.BlockSpec((tt * N_RWKV, LANES), lambda i: (i, 0)), tile, tile, row, row,
                  pl.BlockSpec(memory_space=pl.ANY)],
        out_specs=pl.BlockSpec((tt, d), lambda i: (rb + i, 1)),
        input_output_aliases={5: 0},
        compiler_params=_params(("parallel",)),
        name="rwkv_post",
    )(o_scan, bonus, g, lnx_w, lnx_b, mix)


def _channel_major(a, axis=-1):
    axis = axis % a.ndim
    h = a.shape[axis] // N_RWKV
    y = a.reshape(a.shape[:axis] + (h, N_RWKV) + a.shape[axis + 1:])
    return jnp.swapaxes(y, axis, axis + 1).reshape(a.shape)


def _head_major(a, axis=-1):
    axis = axis % a.ndim
    h = a.shape[axis] // N_RWKV
    y = a.reshape(a.shape[:axis] + (N_RWKV, h) + a.shape[axis + 1:])
    return jnp.swapaxes(y, axis, axis + 1).reshape(a.shape)


def _state_to_scan(s):
    b, h, nv, nk = s.shape
    return s.reshape(b, h, nv, nk // 4, 4).transpose(0, 3, 2, 4, 1).reshape(b, nk // 4, nv, 4 * h)


def _state_from_scan(s):
    b, nslab, nv, lanes = s.shape
    h = lanes // 4
    return s.reshape(b, nslab, nv, 4, h).transpose(0, 4, 2, 1, 3).reshape(b, h, nv, nslab * 4)


def _rkv_channel_major(a, d_rwkv):
    parts = [_channel_major(a[..., i * d_rwkv:(i + 1) * d_rwkv]) for i in range(3)]
    return jnp.concatenate(parts + [a[..., 3 * d_rwkv:]], axis=-1)


def _rkv_head_major(a, d_rwkv):
    parts = [_head_major(a[..., i * d_rwkv:(i + 1) * d_rwkv]) for i in range(3)]
    return jnp.concatenate(parts + [a[..., 3 * d_rwkv:]], axis=-1)


def _pad_rwkv_cols(a, d_rwkv):
    z = jnp.zeros(a.shape[:-1] + (LORA_A_OFF - DECAY_LORA,), a.dtype)
    o = 3 * d_rwkv
    return jnp.concatenate([a[..., :o + DECAY_LORA], z,
                            a[..., o + DECAY_LORA:o + DECAY_LORA + AAA_LORA], z,
                            a[..., o + DECAY_LORA + AAA_LORA:]], axis=-1)


def _unpad_rwkv_cols(a, d_rwkv):
    o = 3 * d_rwkv
    return jnp.concatenate([a[..., :o + DECAY_LORA],
                            a[..., o + LORA_A_OFF:o + LORA_A_OFF + AAA_LORA],
                            a[..., o + LORA_G_OFF:]], axis=-1)


def _pad_rows(w, rows):
    return jnp.concatenate([w, jnp.zeros((rows - w.shape[0], w.shape[1]), w.dtype)], axis=0)


def kernel(x_prompt, x_sample, cache_k, cache_v, state_shift, state_wkv, norm_mix, w_in, lam_q1, lam_k1, lam_q2, lam_k2, attn_subln, shift_mu, decay_w0, decay_up, iclr_a0, iclr_up, gate_up, key_kk, key_ka, bonus_rk, lnx_w, lnx_b, w_out, norm_ffn, ffn_up, ffn_down, norm_final):
    bp, tp, d_model = x_prompt.shape
    bs, ts, _ = x_sample.shape
    depth = w_in.shape[0]
    d_att = d_model // 2
    h_att = d_att // HEAD_W
    d_rwkv = d_model - d_att
    h_rwkv = d_rwkv // N_RWKV
    att_cols = 3 * d_att
    n_p, n_s = bp * tp, bs * ts
    n = n_p + n_s
    past = cache_k.shape[2]
    assert tp % SEG == 0 and ts % SEG == 0 and SEG == CHUNK

    x = jnp.concatenate([x_prompt.reshape(n_p, d_model), x_sample.reshape(n_s, d_model)], axis=0)
    rkv_cols = att_cols + 3 * d_rwkv
    w_lora = _pad_rwkv_cols(w_in[:, :, rkv_cols:], 0)
    w_rkv = _rkv_channel_major(w_in[:, :, att_cols:rkv_cols].astype(BF16), d_rwkv)
    w_out_cm = jnp.concatenate([w_out[:, :d_att].astype(BF16),
                                _channel_major(w_out[:, d_att:].astype(BF16), axis=1)], axis=1)
    row_cm = lambda a: _channel_major(a.reshape(-1)).reshape(1, -1)

    seg_p = jnp.arange(n_p // SEG) % (tp // SEG) == 0
    seg_s = jnp.arange(n_s // SEG) % (ts // SEG) == 0

    outs = {name: [] for name in ("sp", "wp", "ss", "ws")}
    kp = jnp.zeros((depth, n_p, d_att), F32)
    vp = jnp.zeros((depth, n_p, d_att), F32)
    ks = jnp.zeros((depth, n_s, d_att), F32)
    vs = jnp.zeros((depth, n_s, d_att), F32)
    mix = jnp.zeros((n, d_model), BF16)
    for l in range(depth):
        lam_init = 0.8 - 0.6 * math.exp(-0.3 * l)
        xn = _rmsnorm(x, norm_mix[l], BF16)
        q = _matmul(xn, w_in, l, col0=0, ncols=d_att, scale=D_HEAD ** -0.5, out_dtype=BF16, name="w_in_q")
        kp = _matmul_stacked(xn, w_in, l, kp, col0=d_att, row0=0, name="w_in_k")
        vp = _matmul_stacked(xn, w_in, l, vp, col0=2 * d_att, row0=0, name="w_in_v")
        ks = _matmul_stacked(xn, w_in, l, ks, col0=d_att, row0=n_p, name="w_in_ks")
        vs = _matmul_stacked(xn, w_in, l, vs, col0=2 * d_att, row0=n_p, name="w_in_vs")
        rkv = _matmul(xn, w_rkv, l, name="w_in_rkv")
        lora = _matmul(xn, w_lora, l, name="w_in_lora")

        lam_p = jnp.stack([lam_q1[l], lam_k1[l], lam_q2[l], lam_k2[l]])
        subln = attn_subln[l].reshape(1, HEAD_W)
        mix = _attn_prompt(q, kp, vp, l, lam_p, subln, mix, batch=bp, seq=tp, lam_init=lam_init)
        mix = _attn_sample(q, ks, vs, cache_k, cache_v, l, lam_p, subln, mix, row0=n_p, batch=bs, seq=ts,
                           lam_init=lam_init)

        u_last = jnp.concatenate([rkv[SEG - 1::SEG], lora[SEG - 1::SEG]], axis=1)
        prev = jnp.concatenate([jnp.zeros_like(u_last[:1]), u_last[:-1]], axis=0)
        st_p = jnp.zeros((n_p // SEG, u_last.shape[-1]), F32)
        shift_in = _rkv_channel_major(_pad_rwkv_cols(state_shift[l][:, 0, :], d_rwkv), d_rwkv)
        st_s = jnp.repeat(shift_in, ts // SEG, axis=0)
        start = jnp.concatenate([seg_p, seg_s])[:, None]
        halo = jnp.where(start, jnp.concatenate([st_p, st_s], axis=0), prev)
        halo = halo.reshape(n // PREP_ROWS, PREP_ROWS // SEG, halo.shape[-1])

        mu = _rkv_channel_major(_pad_rwkv_cols(shift_mu[l], d_rwkv), d_rwkv).reshape(1, -1)
        r, w, k, v_scan, kk, b, bonus, g = _rwkv_prep(
            rkv, lora, halo, mu,
            row_cm(decay_w0[l]), _pad_rows(_channel_major(decay_up[l]), LORA_A_OFF - LORA_W_OFF),
            row_cm(iclr_a0[l]), _pad_rows(_channel_major(iclr_up[l]), LORA_G_OFF - LORA_A_OFF),
            _channel_major(gate_up[l]), row_cm(key_kk[l]), row_cm(key_ka[l]), row_cm(bonus_rk[l]))
        v_scan = v_scan.reshape(n, N_RWKV, LANES)

        scan_ops = (r, w, k, kk, b, v_scan)
        s0_p = jnp.zeros((bp, d_rwkv // LANES, N_RWKV, LANES), F32)
        o_p, s_p = _rwkv_scan(*scan_ops, s0_p, row0=0, batch=bp, seq=tp)
        o_s, s_s = _rwkv_scan(*scan_ops, _state_to_scan(state_wkv[l]), row0=n_p, batch=bs, seq=ts)
        post_w = (row_cm(lnx_w[l]), row_cm(lnx_b[l]))
        mix = _rwkv_post(o_p.reshape(n_p * N_RWKV, LANES), bonus, g, *post_w, mix, row0=0)
        mix = _rwkv_post(o_s.reshape(n_s * N_RWKV, LANES), bonus, g, *post_w, mix, row0=n_p)

        x = _matmul(mix, w_out_cm, l, residual=x, name="w_out")
        hn = _rmsnorm(x, norm_ffn[l], BF16)
        hid = _matmul(hn, ffn_up, l, act="relu2", out_dtype=BF16, name="ffn_up")
        x = _matmul(hid, ffn_down, l, residual=x, name="ffn_down")

        u_fin = _rkv_head_major(u_last, d_rwkv).reshape(-1, 1, u_last.shape[-1])
        outs["sp"].append(_unpad_rwkv_cols(u_fin[tp // SEG - 1:n_p // SEG:tp // SEG], d_rwkv))
        outs["wp"].append(_state_from_scan(s_p))
        outs["ss"].append(_unpad_rwkv_cols(u_fin[n_p // SEG + ts // SEG - 1::ts // SEG], d_rwkv))
        outs["ws"].append(_state_from_scan(s_s))

    y_p = _rmsnorm(x, norm_final, F32, row0=0, rows=n_p)
    y_s = _rmsnorm(x, norm_final, F32, row0=n_p, rows=n_s)
    return (y_p.reshape(bp, tp, d_model), y_s.reshape(bs, ts, d_model),
            kp.reshape(depth, bp, tp, h_att, HEAD_W), vp.reshape(depth, bp, tp, h_att, HEAD_W),
            jnp.stack(outs["sp"]), jnp.stack(outs["wp"]),
            ks.reshape(depth, bs, ts, h_att, HEAD_W), vs.reshape(depth, bs, ts, h_att, HEAD_W),
            jnp.stack(outs["ss"]), jnp.stack(outs["ws"]))
```

```python
import functools
import math

import jax
import jax.numpy as jnp
from jax import lax
from jax.experimental import pallas as pl
from jax.experimental.pallas import tpu as pltpu

F32 = jnp.float32
BF16 = jnp.bfloat16

D_HEAD = 128
HEAD_W = 2 * D_HEAD
CHUNK = 64
N_RWKV = 64
DECAY_LORA = 96
AAA_LORA = 96
GATE_LORA = 256
RMS_EPS = 1e-5
LN_X_EPS = 64e-5
NEG = -1e30

LANES = 128
SUBLANES = 8
VMEM_LIMIT = 56 * 1024 * 1024

LORA_W_OFF, LORA_A_OFF, LORA_G_OFF, LORA_PAD = 0, 128, 256, 512
SEG = 64
PREP_ROWS = 128
SCAN_CHUNK = 32


def _pick(n, pref, mult):
    best = None
    for d in range(mult, min(n, pref) + 1, mult):
        if n % d == 0:
            best = d
    assert best is not None, (n, pref, mult)
    return best


def _params(sem, flags=None):
    return pltpu.CompilerParams(dimension_semantics=sem, vmem_limit_bytes=VMEM_LIMIT, flags=flags)


def _rmsnorm_kernel(x_ref, g_ref, o_ref):
    x = x_ref[...]
    ms = jnp.mean(x * x, axis=-1, keepdims=True)
    o_ref[...] = (x * lax.rsqrt(ms + RMS_EPS) * g_ref[...]).astype(o_ref.dtype)


def _rmsnorm(x, g, out_dtype, row0=0, rows=None):
    n, d = x.shape
    rows = n - row0 if rows is None else rows
    tr = _pick(math.gcd(rows, row0) or rows, 256, SUBLANES)
    rb = row0 // tr
    return pl.pallas_call(
        _rmsnorm_kernel,
        out_shape=jax.ShapeDtypeStruct((rows, d), out_dtype),
        grid=(rows // tr,),
        in_specs=[pl.BlockSpec((tr, d), lambda i: (rb + i, 0)),
                  pl.BlockSpec((1, d), lambda i: (0, 0))],
        out_specs=pl.BlockSpec((tr, d), lambda i: (i, 0)),
        compiler_params=_params(("parallel",)),
        name="rmsnorm",
    )(x, g.reshape(1, d))


def _mm_kernel(*refs, nk, has_res, has_alias, act, scale):
    refs = list(refs)
    x_ref, w_ref = refs[0], refs[1]
    r_ref = refs[2] if has_res else None
    o_ref = refs[2 + has_res + has_alias]
    acc_ref = refs[-1] if nk > 1 else None

    def epilogue(y):
        if act == "relu2":
            y = jnp.square(jnp.maximum(y, 0.0))
        if scale is not None:
            y = y * scale
        if has_res:
            y = y + r_ref[...]
        o_ref[...] = y.astype(o_ref.dtype).reshape(o_ref.shape)

    if nk == 1:
        epilogue(jnp.dot(x_ref[...], w_ref[...].astype(BF16), preferred_element_type=F32))
        return
    k = pl.program_id(2)

    @pl.when(k == 0)
    def _():
        acc_ref[...] = jnp.zeros_like(acc_ref)

    acc_ref[...] += jnp.dot(x_ref[...], w_ref[...].astype(BF16), preferred_element_type=F32)

    @pl.when(k == nk - 1)
    def _():
        epilogue(acc_ref[...])


FULL_K = 4096


def _mm_tiles(kd):
    return (1024, 512, kd) if kd <= FULL_K else (1024, 1024, 2048)


def _matmul(x, w3, layer, *, col0=0, ncols=None, residual=None, act=None, scale=None, out_dtype=F32,
            name="matmul"):
    m, kd = x.shape
    ncols = w3.shape[2] - col0 if ncols is None else ncols
    tm, tn, tk = _mm_tiles(kd)
    tm, tn, tk = _pick(m, tm, SUBLANES), _pick(math.gcd(ncols, col0) or ncols, tn, LANES), _pick(kd, tk, LANES)
    nk = kd // tk
    cb = col0 // tn
    in_specs = [pl.BlockSpec((tm, tk), lambda i, j, k: (i, k)),
                pl.BlockSpec((None, tk, tn), lambda i, j, k: (layer, k, cb + j))]
    args = [x, w3]
    if residual is not None:
        in_specs.append(pl.BlockSpec((tm, tn), lambda i, j, k: (i, j)))
        args.append(residual)
    return pl.pallas_call(
        functools.partial(_mm_kernel, nk=nk, has_res=residual is not None, has_alias=False, act=act, scale=scale),
        out_shape=jax.ShapeDtypeStruct((m, ncols), out_dtype),
        grid=(m // tm, ncols // tn, nk),
        in_specs=in_specs,
        out_specs=pl.BlockSpec((tm, tn), lambda i, j, k: (i, j)),
        scratch_shapes=[pltpu.VMEM((tm, tn), F32)] if nk > 1 else [],
        compiler_params=_params(("parallel", "parallel", "arbitrary")),
        name=name,
    )(*args)


def _matmul_stacked(x, w3, layer, stack, *, col0, row0, name="matmul_stacked"):
    m, kd = x.shape
    depth, rows, ncols = stack.shape
    tm, tn, tk = _mm_tiles(kd)
    tm, tn, tk = _pick(math.gcd(rows, row0) or rows, tm, SUBLANES), _pick(math.gcd(ncols, col0), tn, LANES), \
        _pick(kd, tk, LANES)
    nk = kd // tk
    rb, cb = row0 // tm, col0 // tn
    return pl.pallas_call(
        functools.partial(_mm_kernel, nk=nk, has_res=False, has_alias=True, act=None, scale=None),
        out_shape=jax.ShapeDtypeStruct(stack.shape, stack.dtype),
        grid=(rows // tm, ncols // tn, nk),
        in_specs=[pl.BlockSpec((tm, tk), lambda i, j, k: (rb + i, k)),
                  pl.BlockSpec((None, tk, tn), lambda i, j, k: (layer, k, cb + j)),
                  pl.BlockSpec(memory_space=pl.ANY)],
        out_specs=pl.BlockSpec((None, tm, tn), lambda i, j, k: (layer, i, j)),
        scratch_shapes=[pltpu.VMEM((tm, tn), F32)] if nk > 1 else [],
        input_output_aliases={2: 0},
        compiler_params=_params(("parallel", "parallel", "arbitrary")),
        name=name,
    )(x, w3, stack)


def _softmax_tile_update(qt_ref, kb, vt, mask, m_ref, l_ref, acc_ref):
    for br in range(2):
        km = kb[:, br * D_HEAD:(br + 1) * D_HEAD]
        st = jnp.dot(km, qt_ref[br * D_HEAD:(br + 1) * D_HEAD, :], preferred_element_type=F32)
        if mask is not None:
            st = jnp.where(mask, st, NEG)
        m_prev = m_ref[br]
        m_new = jnp.maximum(m_prev, jnp.max(st, axis=0, keepdims=True))
        alpha = jnp.exp(m_prev - m_new)
        p = jnp.exp(st - m_new)
        l_ref[br] = alpha * l_ref[br] + jnp.sum(p, axis=0, keepdims=True)
        acc_ref[br] = alpha * acc_ref[br] + jnp.dot(vt, p.astype(BF16), preferred_element_type=F32)
        m_ref[br] = m_new


def _attn_init(q_ref, qt_ref, m_ref, l_ref, acc_ref):
    qt_ref[...] = q_ref[...].astype(F32).T.astype(BF16)
    m_ref[...] = jnp.full(m_ref.shape, NEG, F32)
    l_ref[...] = jnp.zeros(l_ref.shape, F32)
    acc_ref[...] = jnp.zeros(acc_ref.shape, F32)


def _attn_finish(lam_ref, subln_ref, o_ref, l_ref, acc_ref, lam_init):
    lp = lam_ref[...]
    lam = (jnp.exp(jnp.sum(lp[0:1] * lp[1:2], axis=-1, keepdims=True))
           - jnp.exp(jnp.sum(lp[2:3] * lp[3:4], axis=-1, keepdims=True)) + lam_init)
    ot = acc_ref[0] / l_ref[0] - lam * (acc_ref[1] / l_ref[1])
    ot = ot * lax.rsqrt(jnp.mean(ot * ot, axis=0, keepdims=True) + RMS_EPS)
    o_ref[...] = (ot.T * subln_ref[...] * (1.0 - lam_init)).astype(o_ref.dtype)


def _attn_prompt_kernel(lam_ref, subln_ref, q_ref, k_ref, v_ref, mix_in_ref, mix_ref, kb_ref, vt_ref, qt_ref,
                        m_ref, l_ref, acc_ref, *, tq, lam_init):
    del mix_in_ref
    i = pl.program_id(2)
    ntiles = kb_ref.shape[0]

    @pl.when(i == 0)
    def _():
        for j in range(ntiles):
            kb_ref[j] = k_ref[j * tq:(j + 1) * tq, :].astype(BF16)
            vt_ref[j] = v_ref[j * tq:(j + 1) * tq, :].T.astype(BF16)

    _attn_init(q_ref, qt_ref, m_ref, l_ref, acc_ref)

    def body(j, carry):
        _softmax_tile_update(qt_ref, kb_ref[j], vt_ref[j], None, m_ref, l_ref, acc_ref)
        return carry

    lax.fori_loop(0, i, body, 0)
    key_chunk = lax.broadcasted_iota(jnp.int32, (tq, tq), 0) // CHUNK
    qry_chunk = lax.broadcasted_iota(jnp.int32, (tq, tq), 1) // CHUNK
    _softmax_tile_update(qt_ref, kb_ref[i], vt_ref[i], key_chunk <= qry_chunk, m_ref, l_ref, acc_ref)
    _attn_finish(lam_ref, subln_ref, mix_ref, l_ref, acc_ref, lam_init)


def _attn_scratch(tq):
    return [pltpu.VMEM((HEAD_W, tq), BF16),
            pltpu.VMEM((2, 1, tq), F32),
            pltpu.VMEM((2, 1, tq), F32),
            pltpu.VMEM((2, HEAD_W, tq), F32)]


def _attn_prompt(q, k3, v3, layer, lam_p, subln, mix, *, batch, seq, lam_init):
    h_att = k3.shape[2] // HEAD_W
    tq = _pick(seq, 1024, CHUNK)
    nq = seq // tq
    kv_spec = pl.BlockSpec((None, seq, HEAD_W), lambda b, h, i: (layer, b, h))
    return pl.pallas_call(
        functools.partial(_attn_prompt_kernel, tq=tq, lam_init=lam_init),
        out_shape=jax.ShapeDtypeStruct(mix.shape, mix.dtype),
        grid=(batch, h_att, nq),
        in_specs=[pl.BlockSpec((4, D_HEAD), lambda b, h, i: (0, 0)),
                  pl.BlockSpec((1, HEAD_W), lambda b, h, i: (0, 0)),
                  pl.BlockSpec((tq, HEAD_W), lambda b, h, i: (b * nq + i, h)),
                  kv_spec, kv_spec, pl.BlockSpec(memory_space=pl.ANY)],
        out_specs=pl.BlockSpec((tq, HEAD_W), lambda b, h, i: (b * nq + i, h)),
        input_output_aliases={5: 0},
        scratch_shapes=[pltpu.VMEM((nq, tq, HEAD_W), BF16), pltpu.VMEM((nq, HEAD_W, tq), BF16)] + _attn_scratch(tq),
        compiler_params=_params(("parallel", "parallel", "arbitrary")),
        name="attn_prompt",
    )(lam_p, subln, q, k3, v3, mix)


def _softmax_rows_update(qh, kbs, vb, m_ref, l_ref, acc_ref, slot):
    for br in range(2):
        s = lax.dot_general(qh[:, br * D_HEAD:(br + 1) * D_HEAD], kbs[br],
                            (((1,), (1,)), ((), ())), preferred_element_type=F32)
        m_prev = m_ref[slot + br]
        m_new = jnp.maximum(m_prev, jnp.max(s, axis=-1, keepdims=True))
        alpha = jnp.exp(m_prev - m_new)
        p = jnp.exp(s - m_new)
        l_ref[slot + br] = alpha * l_ref[slot + br] + jnp.sum(p, axis=-1, keepdims=True)
        acc_ref[slot + br] = alpha * acc_ref[slot + br] + jnp.dot(p.astype(BF16), vb, preferred_element_type=F32)
        m_ref[slot + br] = m_new


def _attn_sample_kernel(lam_ref, subln_ref, q_ref, kn_ref, vn_ref, k1_ref, k2_ref, v1_ref, v2_ref, mix_in_ref,
                        mix_ref, m_ref, l_ref, acc_ref, *, tp, h_att, nchunks, lam_init):
    del mix_in_ref
    c = pl.program_id(1)

    @pl.when(c == 0)
    def _():
        m_ref[...] = jnp.full(m_ref.shape, NEG, F32)
        l_ref[...] = jnp.zeros(l_ref.shape, F32)
        acc_ref[...] = jnp.zeros(acc_ref.shape, F32)

    def head_rows(ref, h):
        return ref[pl.ds(h, tp, stride=h_att), :].astype(BF16)

    for h in range(h_att):
        vb = jnp.concatenate([head_rows(v1_ref, h), head_rows(v2_ref, h)], axis=1)
        _softmax_rows_update(q_ref[:, h * HEAD_W:(h + 1) * HEAD_W], (head_rows(k1_ref, h), head_rows(k2_ref, h)),
                             vb, m_ref, l_ref, acc_ref, 2 * h)

    @pl.when(c == nchunks - 1)
    def _():
        lp = lam_ref[...]
        lam = (jnp.exp(jnp.sum(lp[0:1] * lp[1:2], axis=-1, keepdims=True))
               - jnp.exp(jnp.sum(lp[2:3] * lp[3:4], axis=-1, keepdims=True)) + lam_init)
        for h in range(h_att):
            cols = slice(h * HEAD_W, (h + 1) * HEAD_W)
            kn = kn_ref[:, cols].astype(BF16)
            _softmax_rows_update(q_ref[:, cols], (kn[:, :D_HEAD], kn[:, D_HEAD:]), vn_ref[:, cols].astype(BF16),
                                 m_ref, l_ref, acc_ref, 2 * h)
            o = acc_ref[2 * h] / l_ref[2 * h] - lam * (acc_ref[2 * h + 1] / l_ref[2 * h + 1])
            o = o * lax.rsqrt(jnp.mean(o * o, axis=-1, keepdims=True) + RMS_EPS)
            mix_ref[:, cols] = (o * subln_ref[...] * (1.0 - lam_init)).astype(mix_ref.dtype)


def _attn_sample(q, k3, v3, cache_k, cache_v, layer, lam_p, subln, mix, *, row0, batch, seq, lam_init):
    h_att = cache_k.shape[3]
    d_att = h_att * HEAD_W
    past = cache_k.shape[2]
    assert past % CHUNK == 0 and seq <= CHUNK and row0 % seq == 0
    rb0 = row0 // seq
    tp = _pick(past, 1024, LANES)
    nchunks = past // tp
    new_spec = pl.BlockSpec((None, seq, d_att), lambda b, c: (layer, b, 0))
    cache_k = cache_k.reshape(cache_k.shape[0], batch, past * h_att, HEAD_W)
    cache_v = cache_v.reshape(cache_v.shape[0], batch, past * h_att, HEAD_W)
    past_lo = pl.BlockSpec((None, None, tp * h_att, D_HEAD), lambda b, c: (layer, b, c, 0))
    past_hi = pl.BlockSpec((None, None, tp * h_att, D_HEAD), lambda b, c: (layer, b, c, 1))
    return pl.pallas_call(
        functools.partial(_attn_sample_kernel, tp=tp, h_att=h_att, nchunks=nchunks, lam_init=lam_init),
        out_shape=jax.ShapeDtypeStruct(mix.shape, mix.dtype),
        grid=(batch, nchunks),
        in_specs=[pl.BlockSpec((4, D_HEAD), lambda b, c: (0, 0)),
                  pl.BlockSpec((1, HEAD_W), lambda b, c: (0, 0)),
                  pl.BlockSpec((seq, d_att), lambda b, c: (rb0 + b, 0)),
                  new_spec, new_spec, past_lo, past_hi, past_lo, past_hi,
                  pl.BlockSpec(memory_space=pl.ANY)],
        out_specs=pl.BlockSpec((seq, d_att), lambda b, c: (rb0 + b, 0)),
        input_output_aliases={9: 0},
        scratch_shapes=[pltpu.VMEM((2 * h_att, seq, 1), F32), pltpu.VMEM((2 * h_att, seq, 1), F32),
                        pltpu.VMEM((2 * h_att, seq, HEAD_W), F32)],
        compiler_params=_params(("parallel", "arbitrary")),
        name="attn_sample",
    )(lam_p, subln, q, k3, v3, cache_k, cache_k, cache_v, cache_v, mix)


def _dot_f32(a, b):
    return jnp.dot(a, b, precision=lax.Precision.HIGHEST, preferred_element_type=F32)


def _lane_group_sum(x):
    x = x + pltpu.roll(x, 2 * (LANES // 4), 1)
    return x + pltpu.roll(x, LANES // 4, 1)


def _head_sum(x):
    acc = x[:, 0:LANES]
    for j in range(1, x.shape[1] // LANES):
        acc = acc + x[:, j * LANES:(j + 1) * LANES]
    return _lane_group_sum(acc)


def _tile_lanes(slab, width):
    return jnp.concatenate([slab] * (width // LANES), axis=1)


def _rwkv_prep_kernel(ur_ref, uk_ref, uv_ref, ul_ref, halo_ref, mu_ref, w0_ref, wup_ref, a0_ref, aup_ref,
                      gup_ref, kkw_ref, kaw_ref, rkw_ref,
                      r_out, w_out, k_out, vs_out, kk_out, b_out, bonus_out, g_out, *, tt, d_rwkv):
    nseg = tt // SEG
    first_row = lax.broadcasted_iota(jnp.int32, (SEG, 1), 0) == 0

    def shifted(u_ref, col0, width):
        parts = []
        for s in range(nseg):
            u = u_ref[s * SEG:(s + 1) * SEG, :]
            prev = jnp.where(first_row, halo_ref[s:s + 1, col0:col0 + width], pltpu.roll(u, 1, 0))
            parts.append(u + (prev - u) * mu_ref[:, col0:col0 + width])
        return jnp.concatenate(parts, axis=0) if nseg > 1 else parts[0]

    r = shifted(ur_ref, 0, d_rwkv)
    k = shifted(uk_ref, d_rwkv, d_rwkv)
    v = shifted(uv_ref, 2 * d_rwkv, d_rwkv)
    lo = shifted(ul_ref, 3 * d_rwkv, LORA_PAD)
    wd = lo[:, LORA_W_OFF:LORA_A_OFF]
    ad = lo[:, LORA_A_OFF:LORA_G_OFF]
    gd = lo[:, LORA_G_OFF:LORA_PAD]

    w = w0_ref[...] + _dot_f32(jnp.tanh(wd), wup_ref[...])
    x = -w
    w = -(jnp.maximum(x, 0.0) + jnp.log1p(jnp.exp(-jnp.abs(x)))) - 0.5
    rate = jnp.exp(w)
    pos = lax.broadcasted_iota(jnp.int32, (tt, 1), 0) % SCAN_CHUNK
    cum = rate
    shift = 1
    while shift < SCAN_CHUNK:
        cum = cum + jnp.where(pos >= shift, pltpu.roll(cum, shift, 0), 0.0)
        shift *= 2
    decay_to = jnp.exp(-cum)
    decay_before = jnp.exp(rate - cum)
    undo_decay = jnp.exp(cum)
    a = jax.nn.sigmoid(a0_ref[...] + _dot_f32(ad, aup_ref[...]))
    g = _dot_f32(jax.nn.sigmoid(gd), gup_ref[...])

    kk = k * kkw_ref[...]
    inv = 1.0 / jnp.maximum(jnp.sqrt(_head_sum(kk * kk)), 1e-12)
    kk = kk * _tile_lanes(inv, d_rwkv)
    kmod = k * (1.0 + (a - 1.0) * kaw_ref[...])
    coef = _head_sum(r * kmod * rkw_ref[...])

    r_out[...] = r * decay_to
    w_out[...] = decay_to
    k_out[...] = kmod * undo_decay
    kk_out[...] = kk * decay_before
    b_out[...] = kk * a * undo_decay
    bonus_out[...] = _tile_lanes(coef, d_rwkv) * v
    g_out[...] = g

    group = lax.broadcasted_iota(jnp.int32, (tt, LANES), 1) // (LANES // 4)
    for j in range(d_rwkv // LANES):
        slab = v[:, j * LANES:(j + 1) * LANES]
        for gi in range(4):
            vs_out[pl.ds(4 * j + gi, tt, stride=N_RWKV), :] = _lane_group_sum(jnp.where(group == gi, slab, 0.0))


def _rwkv_prep(rkv, lora, halo, mu, w0, wup, a0, aup, gup, kkw, kaw, rkw):
    n = rkv.shape[0]
    d_rwkv = rkv.shape[1] // 3
    tt = PREP_ROWS
    assert n % tt == 0 and d_rwkv // N_RWKV == LANES // 4
    row = lambda width: pl.BlockSpec((1, width), lambda i: (0, 0))
    full = lambda a: pl.BlockSpec(a.shape, lambda i: (0, 0))
    tile = pl.BlockSpec((tt, d_rwkv), lambda i: (i, 0))
    nat = jax.ShapeDtypeStruct((n, d_rwkv), F32)
    return pl.pallas_call(
        functools.partial(_rwkv_prep_kernel, tt=tt, d_rwkv=d_rwkv),
        out_shape=[nat, nat, nat, jax.ShapeDtypeStruct((n * N_RWKV, LANES), F32), nat, nat, nat, nat],
        grid=(n // tt,),
        in_specs=[pl.BlockSpec((tt, d_rwkv), lambda i: (i, 0)),
                  pl.BlockSpec((tt, d_rwkv), lambda i: (i, 1)),
                  pl.BlockSpec((tt, d_rwkv), lambda i: (i, 2)),
                  pl.BlockSpec((tt, LORA_PAD), lambda i: (i, 0)),
                  pl.BlockSpec((None, tt // SEG, halo.shape[-1]), lambda i: (i, 0, 0)),
                  row(mu.shape[-1]), row(d_rwkv), full(wup), row(d_rwkv), full(aup), full(gup),
                  row(d_rwkv), row(d_rwkv), row(d_rwkv)],
        out_specs=[tile, tile, tile, pl.BlockSpec((tt * N_RWKV, LANES), lambda i: (i, 0)), tile, tile, tile, tile],
        compiler_params=_params(("parallel",)),
        name="rwkv_prep",
    )(rkv, rkv, rkv, lora, halo, mu, w0, wup, a0, aup, gup, kkw, kaw, rkw)


SCAN_SEQS = 2
SCAN_VROWS = 64
SCAN_PARTIALS = 4


def _rwkv_scan_kernel(*refs, tc, nchunks):
    nops = 5
    op_refs = [refs[e * nops:(e + 1) * nops] for e in range(SCAN_SEQS)]
    v_refs = refs[SCAN_SEQS * nops:SCAN_SEQS * (nops + 1)]
    s0_ref, o_ref, sout_ref, s_ref = refs[SCAN_SEQS * (nops + 1):]
    c = pl.program_id(1)
    nslab = s_ref.shape[1]

    @pl.when(c == 0)
    def _():
        s_ref[...] = s0_ref[...]

    vparts = N_RWKV // SCAN_VROWS

    def token(t8, i, e, p):
        base = pl.multiple_of(t8 * SUBLANES, SUBLANES)
        r_ref, _, k_ref, kk_ref, b_ref = (ref.at[pl.ds(base, SUBLANES), :] for ref in op_refs[e])
        vrows = slice(p * SCAN_VROWS, (p + 1) * SCAN_VROWS)

        def row(ref, j):
            return ref[i:i + 1, j * LANES:(j + 1) * LANES]

        vt = v_refs[e][base + i, vrows, :]
        sa = [None] * SCAN_PARTIALS
        for j in range(nslab):
            term = s_ref[e, j, vrows, :] * row(kk_ref, j)
            sa[j % SCAN_PARTIALS] = term if sa[j % SCAN_PARTIALS] is None else sa[j % SCAN_PARTIALS] + term
        sa = _lane_group_sum(functools.reduce(lambda a, c: a + c, sa))
        oo = [None] * SCAN_PARTIALS
        for j in range(nslab):
            s_new = s_ref[e, j, vrows, :] + (vt * row(k_ref, j) - sa * row(b_ref, j))
            s_ref[e, j, vrows, :] = s_new
            term = s_new * row(r_ref, j)
            oo[j % SCAN_PARTIALS] = term if oo[j % SCAN_PARTIALS] is None else oo[j % SCAN_PARTIALS] + term
        o_ref[e, base + i, vrows, :] = _lane_group_sum(functools.reduce(lambda a, c: a + c, oo))

    def step8(t8, carry):
        for i in range(SUBLANES):
            for e in range(SCAN_SEQS):
                for p in range(vparts):
                    token(t8, i, e, p)
        return carry

    lax.fori_loop(0, tc // SUBLANES, step8, 0)

    for e in range(SCAN_SEQS):
        decay_ref = op_refs[e][1]
        for j in range(nslab):
            s_ref[e, j] = s_ref[e, j] * decay_ref[tc - 1:tc, j * LANES:(j + 1) * LANES]

    @pl.when(c == nchunks - 1)
    def _():
        sout_ref[...] = s_ref[...]


def _rwkv_scan(r, w, k, kk, b, v_scan, s0, *, row0, batch, seq):
    d_rwkv = r.shape[1]
    nslab = d_rwkv // LANES
    tc = SCAN_CHUNK
    nchunks = seq // tc
    assert batch % SCAN_SEQS == 0 and row0 % tc == 0 and seq % tc == 0 and PREP_ROWS % tc == 0
    op_specs, v_specs = [], []
    for e in range(SCAN_SEQS):
        first = lambda g, c, e=e: (row0 + (SCAN_SEQS * g + e) * seq) // tc + c
        op_specs += [pl.BlockSpec((tc, d_rwkv), lambda g, c, f=first: (f(g, c), 0))] * 5
        v_specs.append(pl.BlockSpec((tc, N_RWKV, LANES), lambda g, c, f=first: (f(g, c), 0, 0)))
    state_spec = pl.BlockSpec((SCAN_SEQS, nslab, N_RWKV, LANES), lambda g, c: (g, 0, 0, 0))
    return pl.pallas_call(
        functools.partial(_rwkv_scan_kernel, tc=tc, nchunks=nchunks),
        out_shape=[jax.ShapeDtypeStruct((batch, seq, N_RWKV, LANES), F32),
                   jax.ShapeDtypeStruct((batch, nslab, N_RWKV, LANES), F32)],
        grid=(batch // SCAN_SEQS, nchunks),
        in_specs=op_specs + v_specs + [state_spec],
        out_specs=[pl.BlockSpec((SCAN_SEQS, tc, N_RWKV, LANES), lambda g, c: (g, c, 0, 0)), state_spec],
        scratch_shapes=[pltpu.VMEM((SCAN_SEQS, nslab, N_RWKV, LANES), F32)],
        compiler_params=_params(("parallel", "arbitrary")),
        name="rwkv_scan",
    )(*([r, w, k, kk, b] * SCAN_SEQS), *([v_scan] * SCAN_SEQS), s0)


def _rwkv_post_kernel(o_ref, bonus_ref, g_ref, lw_ref, lb_ref, mix_in_ref, out_ref, *, tt):
    del mix_in_ref
    d = bonus_ref.shape[1]
    group = lax.broadcasted_iota(jnp.int32, (tt, LANES), 1) // (LANES // 4)
    slabs = []
    for j in range(d // LANES):
        ch = [o_ref[pl.ds(4 * j + gi, tt, stride=N_RWKV), :] for gi in range(4)]
        slabs.append(jnp.where(group == 0, ch[0], jnp.where(group == 1, ch[1], jnp.where(group == 2, ch[2], ch[3]))))
    o = jnp.concatenate(slabs, axis=1)
    inv_n = 1.0 / N_RWKV
    dev = o - _tile_lanes(_head_sum(o) * inv_n, d)
    rstd = lax.rsqrt(_head_sum(dev * dev) * inv_n + LN_X_EPS)
    y = dev * _tile_lanes(rstd, d) * lw_ref[...] + lb_ref[...]
    out_ref[...] = ((y + bonus_ref[...]) * g_ref[...]).astype(out_ref.dtype)


def _rwkv_post(o_scan, bonus, g, lnx_w, lnx_b, mix, *, row0):
    rows = o_scan.shape[0] // N_RWKV
    d = bonus.shape[1]
    tt = _pick(math.gcd(rows, row0) or rows, 256, SUBLANES)
    rb = row0 // tt
    tile = pl.BlockSpec((tt, d), lambda i: (rb + i, 0))
    row = pl.BlockSpec((1, d), lambda i: (0, 0))
    return pl.pallas_call(
        functools.partial(_rwkv_post_kernel, tt=tt),
        out_shape=jax.ShapeDtypeStruct(mix.shape, mix.dtype),
        grid=(rows // tt,),
        in_specs=[pl.BlockSpec((tt * N_RWKV, LANES), lambda i: (i, 0)), tile, tile, row, row,
                  pl.BlockSpec(memory_space=pl.ANY)],
        out_specs=pl.BlockSpec((tt, d), lambda i: (rb + i, 1)),
        input_output_aliases={5: 0},
        compiler_params=_params(("parallel",)),
        name="rwkv_post",
    )(o_scan, bonus, g, lnx_w, lnx_b, mix)


def _channel_major(a, axis=-1):
    axis = axis % a.ndim
    h = a.shape[axis] // N_RWKV
    y = a.reshape(a.shape[:axis] + (h, N_RWKV) + a.shape[axis + 1:])
    return jnp.swapaxes(y, axis, axis + 1).reshape(a.shape)


def _head_major(a, axis=-1):
    axis = axis % a.ndim
    h = a.shape[axis] // N_RWKV
    y = a.reshape(a.shape[:axis] + (N_RWKV, h) + a.shape[axis + 1:])
    return jnp.swapaxes(y, axis, axis + 1).reshape(a.shape)


def _state_to_scan(s):
    b, h, nv, nk = s.shape
    return s.reshape(b, h, nv, nk // 4, 4).transpose(0, 3, 2, 4, 1).reshape(b, nk // 4, nv, 4 * h)


def _state_from_scan(s):
    b, nslab, nv, lanes = s.shape
    h = lanes // 4
    return s.reshape(b, nslab, nv, 4, h).transpose(0, 4, 2, 1, 3).reshape(b, h, nv, nslab * 4)


def _rkv_channel_major(a, d_rwkv):
    parts = [_channel_major(a[..., i * d_rwkv:(i + 1) * d_rwkv]) for i in range(3)]
    return jnp.concatenate(parts + [a[..., 3 * d_rwkv:]], axis=-1)


def _rkv_head_major(a, d_rwkv):
    parts = [_head_major(a[..., i * d_rwkv:(i + 1) * d_rwkv]) for i in range(3)]
    return jnp.concatenate(parts + [a[..., 3 * d_rwkv:]], axis=-1)


def _pad_rwkv_cols(a, d_rwkv):
    z = jnp.zeros(a.shape[:-1] + (LORA_A_OFF - DECAY_LORA,), a.dtype)
    o = 3 * d_rwkv
    return jnp.concatenate([a[..., :o + DECAY_LORA], z,
                            a[..., o + DECAY_LORA:o + DECAY_LORA + AAA_LORA], z,
                            a[..., o + DECAY_LORA + AAA_LORA:]], axis=-1)


def _unpad_rwkv_cols(a, d_rwkv):
    o = 3 * d_rwkv
    return jnp.concatenate([a[..., :o + DECAY_LORA],
                            a[..., o + LORA_A_OFF:o + LORA_A_OFF + AAA_LORA],
                            a[..., o + LORA_G_OFF:]], axis=-1)


def _pad_rows(w, rows):
    return jnp.concatenate([w, jnp.zeros((rows - w.shape[0], w.shape[1]), w.dtype)], axis=0)


def kernel(x_prompt, x_sample, cache_k, cache_v, state_shift, state_wkv, norm_mix, w_in, lam_q1, lam_k1, lam_q2, lam_k2, attn_subln, shift_mu, decay_w0, decay_up, iclr_a0, iclr_up, gate_up, key_kk, key_ka, bonus_rk, lnx_w, lnx_b, w_out, norm_ffn, ffn_up, ffn_down, norm_final):
    bp, tp, d_model = x_prompt.shape
    bs, ts, _ = x_sample.shape
    depth = w_in.shape[0]
    d_att = d_model // 2
    h_att = d_att // HEAD_W
    d_rwkv = d_model - d_att
    h_rwkv = d_rwkv // N_RWKV
    att_cols = 3 * d_att
    n_p, n_s = bp * tp, bs * ts
    n = n_p + n_s
    past = cache_k.shape[2]
    assert tp % SEG == 0 and ts % SEG == 0 and SEG == CHUNK

    x = jnp.concatenate([x_prompt.reshape(n_p, d_model), x_sample.reshape(n_s, d_model)], axis=0)
    rkv_cols = att_cols + 3 * d_rwkv
    w_lora = _pad_rwkv_cols(w_in[:, :, rkv_cols:], 0)
    w_rkv = _rkv_channel_major(w_in[:, :, att_cols:rkv_cols].astype(BF16), d_rwkv)
    w_out_cm = jnp.concatenate([w_out[:, :d_att].astype(BF16),
                                _channel_major(w_out[:, d_att:].astype(BF16), axis=1)], axis=1)
    row_cm = lambda a: _channel_major(a.reshape(-1)).reshape(1, -1)

    seg_p = jnp.arange(n_p // SEG) % (tp // SEG) == 0
    seg_s = jnp.arange(n_s // SEG) % (ts // SEG) == 0

    outs = {name: [] for name in ("sp", "wp", "ss", "ws")}
    kp = jnp.zeros((depth, n_p, d_att), F32)
    vp = jnp.zeros((depth, n_p, d_att), F32)
    ks = jnp.zeros((depth, n_s, d_att), F32)
    vs = jnp.zeros((depth, n_s, d_att), F32)
    mix = jnp.zeros((n, d_model), BF16)
    for l in range(depth):
        lam_init = 0.8 - 0.6 * math.exp(-0.3 * l)
        xn = _rmsnorm(x, norm_mix[l], BF16)
        q = _matmul(xn, w_in, l, col0=0, ncols=d_att, scale=D_HEAD ** -0.5, out_dtype=BF16, name="w_in_q")
        kp = _matmul_stacked(xn, w_in, l, kp, col0=d_att, row0=0, name="w_in_k")
        vp = _matmul_stacked(xn, w_in, l, vp, col0=2 * d_att, row0=0, name="w_in_v")
        ks = _matmul_stacked(xn, w_in, l, ks, col0=d_att, row0=n_p, name="w_in_ks")
        vs = _matmul_stacked(xn, w_in, l, vs, col0=2 * d_att, row0=n_p, name="w_in_vs")
        rkv = _matmul(xn, w_rkv, l, name="w_in_rkv")
        lora = _matmul(xn, w_lora, l, name="w_in_lora")

        lam_p = jnp.stack([lam_q1[l], lam_k1[l], lam_q2[l], lam_k2[l]])
        subln = attn_subln[l].reshape(1, HEAD_W)
        mix = _attn_prompt(q, kp, vp, l, lam_p, subln, mix, batch=bp, seq=tp, lam_init=lam_init)
        mix = _attn_sample(q, ks, vs, cache_k, cache_v, l, lam_p, subln, mix, row0=n_p, batch=bs, seq=ts,
                           lam_init=lam_init)

        u_last = jnp.concatenate([rkv[SEG - 1::SEG], lora[SEG - 1::SEG]], axis=1)
        prev = jnp.concatenate([jnp.zeros_like(u_last[:1]), u_last[:-1]], axis=0)
        st_p = jnp.zeros((n_p // SEG, u_last.shape[-1]), F32)
        shift_in = _rkv_channel_major(_pad_rwkv_cols(state_shift[l][:, 0, :], d_rwkv), d_rwkv)
        st_s = jnp.repeat(shift_in, ts // SEG, axis=0)
        start = jnp.concatenate([seg_p, seg_s])[:, None]
        halo = jnp.where(start, jnp.concatenate([st_p, st_s], axis=0), prev)
        halo = halo.reshape(n // PREP_ROWS, PREP_ROWS // SEG, halo.shape[-1])

        mu = _rkv_channel_major(_pad_rwkv_cols(shift_mu[l], d_rwkv), d_rwkv).reshape(1, -1)
        r, w, k, v_scan, kk, b, bonus, g = _rwkv_prep(
            rkv, lora, halo, mu,
            row_cm(decay_w0[l]), _pad_rows(_channel_major(decay_up[l]), LORA_A_OFF - LORA_W_OFF),
            row_cm(iclr_a0[l]), _pad_rows(_channel_major(iclr_up[l]), LORA_G_OFF - LORA_A_OFF),
            _channel_major(gate_up[l]), row_cm(key_kk[l]), row_cm(key_ka[l]), row_cm(bonus_rk[l]))
        v_scan = v_scan.reshape(n, N_RWKV, LANES)

        scan_ops = (r, w, k, kk, b, v_scan)
        s0_p = jnp.zeros((bp, d_rwkv // LANES, N_RWKV, LANES), F32)
        o_p, s_p = _rwkv_scan(*scan_ops, s0_p, row0=0, batch=bp, seq=tp)
        o_s, s_s = _rwkv_scan(*scan_ops, _state_to_scan(state_wkv[l]), row0=n_p, batch=bs, seq=ts)
        post_w = (row_cm(lnx_w[l]), row_cm(lnx_b[l]))
        mix = _rwkv_post(o_p.reshape(n_p * N_RWKV, LANES), bonus, g, *post_w, mix, row0=0)
        mix = _rwkv_post(o_s.reshape(n_s * N_RWKV, LANES), bonus, g, *post_w, mix, row0=n_p)

        x = _matmul(mix, w_out_cm, l, residual=x, name="w_out")
        hn = _rmsnorm(x, norm_ffn[l], BF16)
        hid = _matmul(hn, ffn_up, l, act="relu2", out_dtype=BF16, name="ffn_up")
        x = _matmul(hid, ffn_down, l, residual=x, name="ffn_down")

        u_fin = _rkv_head_major(u_last, d_rwkv).reshape(-1, 1, u_last.shape[-1])
        outs["sp"].append(_unpad_rwkv_cols(u_fin[tp // SEG - 1:n_p // SEG:tp // SEG], d_rwkv))
        outs["wp"].append(_state_from_scan(s_p))
        outs["ss"].append(_unpad_rwkv_cols(u_fin[n_p // SEG + ts // SEG - 1::ts // SEG], d_rwkv))
        outs["ws"].append(_state_from_scan(s_s))

    y_p = _rmsnorm(x, norm_final, F32, row0=0, rows=n_p)
    y_s = _rmsnorm(x, norm_final, F32, row0=n_p, rows=n_s)
    return (y_p.reshape(bp, tp, d_model), y_s.reshape(bs, ts, d_model),
            kp.reshape(depth, bp, tp, h_att, HEAD_W), vp.reshape(depth, bp, tp, h_att, HEAD_W),
            jnp.stack(outs["sp"]), jnp.stack(outs["wp"]),
            ks.reshape(depth, bs, ts, h_att, HEAD_W), vs.reshape(depth, bs, ts, h_att, HEAD_W),
            jnp.stack(outs["ss"]), jnp.stack(outs["ws"]))
```

```python
import functools
import math

import jax
import jax.numpy as jnp
from jax import lax
from jax.experimental import pallas as pl
from jax.experimental.pallas import tpu as pltpu

F32 = jnp.float32
BF16 = jnp.bfloat16

D_HEAD = 128
HEAD_W = 2 * D_HEAD
CHUNK = 64
N_RWKV = 64
DECAY_LORA = 96
AAA_LORA = 96
GATE_LORA = 256
RMS_EPS = 1e-5
LN_X_EPS = 64e-5
NEG = -1e30

LANES = 128
SUBLANES = 8
VMEM_LIMIT = 56 * 1024 * 1024

LORA_W_OFF, LORA_A_OFF, LORA_G_OFF, LORA_PAD = 0, 128, 256, 512
SEG = 64
PREP_ROWS = 128
SCAN_CHUNK = 64


def _pick(n, pref, mult):
    best = None
    for d in range(mult, min(n, pref) + 1, mult):
        if n % d == 0:
            best = d
    assert best is not None, (n, pref, mult)
    return best


def _params(sem, flags=None):
    return pltpu.CompilerParams(dimension_semantics=sem, vmem_limit_bytes=VMEM_LIMIT, flags=flags)


def _rmsnorm_kernel(x_ref, g_ref, o_ref):
    x = x_ref[...]
    ms = jnp.mean(x * x, axis=-1, keepdims=True)
    o_ref[...] = (x * lax.rsqrt(ms + RMS_EPS) * g_ref[...]).astype(o_ref.dtype)


def _rmsnorm(x, g, out_dtype, row0=0, rows=None):
    n, d = x.shape
    rows = n - row0 if rows is None else rows
    tr = _pick(math.gcd(rows, row0) or rows, 256, SUBLANES)
    rb = row0 // tr
    return pl.pallas_call(
        _rmsnorm_kernel,
        out_shape=jax.ShapeDtypeStruct((rows, d), out_dtype),
        grid=(rows // tr,),
        in_specs=[pl.BlockSpec((tr, d), lambda i: (rb + i, 0)),
                  pl.BlockSpec((1, d), lambda i: (0, 0))],
        out_specs=pl.BlockSpec((tr, d), lambda i: (i, 0)),
        compiler_params=_params(("parallel",)),
        name="rmsnorm",
    )(x, g.reshape(1, d))


def _mm_kernel(*refs, nk, has_res, has_alias, act, scale):
    refs = list(refs)
    x_ref, w_ref = refs[0], refs[1]
    r_ref = refs[2] if has_res else None
    o_ref = refs[2 + has_res + has_alias]
    acc_ref = refs[-1] if nk > 1 else None

    def epilogue(y):
        if act == "relu2":
            y = jnp.square(jnp.maximum(y, 0.0))
        if scale is not None:
            y = y * scale
        if has_res:
            y = y + r_ref[...]
        o_ref[...] = y.astype(o_ref.dtype).reshape(o_ref.shape)

    if nk == 1:
        epilogue(jnp.dot(x_ref[...], w_ref[...].astype(BF16), preferred_element_type=F32))
        return
    k = pl.program_id(2)

    @pl.when(k == 0)
    def _():
        acc_ref[...] = jnp.zeros_like(acc_ref)

    acc_ref[...] += jnp.dot(x_ref[...], w_ref[...].astype(BF16), preferred_element_type=F32)

    @pl.when(k == nk - 1)
    def _():
        epilogue(acc_ref[...])


FULL_K = 4096


def _mm_tiles(kd):
    return (1024, 512, kd) if kd <= FULL_K else (1024, 1024, 2048)


def _matmul(x, w3, layer, *, col0=0, ncols=None, residual=None, act=None, scale=None, out_dtype=F32,
            name="matmul"):
    m, kd = x.shape
    ncols = w3.shape[2] - col0 if ncols is None else ncols
    tm, tn, tk = _mm_tiles(kd)
    tm, tn, tk = _pick(m, tm, SUBLANES), _pick(math.gcd(ncols, col0) or ncols, tn, LANES), _pick(kd, tk, LANES)
    nk = kd // tk
    cb = col0 // tn
    in_specs = [pl.BlockSpec((tm, tk), lambda i, j, k: (i, k)),
                pl.BlockSpec((None, tk, tn), lambda i, j, k: (layer, k, cb + j))]
    args = [x, w3]
    if residual is not None:
        in_specs.append(pl.BlockSpec((tm, tn), lambda i, j, k: (i, j)))
        args.append(residual)
    return pl.pallas_call(
        functools.partial(_mm_kernel, nk=nk, has_res=residual is not None, has_alias=False, act=act, scale=scale),
        out_shape=jax.ShapeDtypeStruct((m, ncols), out_dtype),
        grid=(m // tm, ncols // tn, nk),
        in_specs=in_specs,
        out_specs=pl.BlockSpec((tm, tn), lambda i, j, k: (i, j)),
        scratch_shapes=[pltpu.VMEM((tm, tn), F32)] if nk > 1 else [],
        compiler_params=_params(("parallel", "parallel", "arbitrary")),
        name=name,
    )(*args)


def _matmul_stacked(x, w3, layer, stack, *, col0, row0, name="matmul_stacked"):
    m, kd = x.shape
    depth, rows, ncols = stack.shape
    tm, tn, tk = _mm_tiles(kd)
    tm, tn, tk = _pick(math.gcd(rows, row0) or rows, tm, SUBLANES), _pick(math.gcd(ncols, col0), tn, LANES), \
        _pick(kd, tk, LANES)
    nk = kd // tk
    rb, cb = row0 // tm, col0 // tn
    return pl.pallas_call(
        functools.partial(_mm_kernel, nk=nk, has_res=False, has_alias=True, act=None, scale=None),
        out_shape=jax.ShapeDtypeStruct(stack.shape, stack.dtype),
        grid=(rows // tm, ncols // tn, nk),
        in_specs=[pl.BlockSpec((tm, tk), lambda i, j, k: (rb + i, k)),
                  pl.BlockSpec((None, tk, tn), lambda i, j, k: (layer, k, cb + j)),
                  pl.BlockSpec(memory_space=pl.ANY)],
        out_specs=pl.BlockSpec((None, tm, tn), lambda i, j, k: (layer, i, j)),
        scratch_shapes=[pltpu.VMEM((tm, tn), F32)] if nk > 1 else [],
        input_output_aliases={2: 0},
        compiler_params=_params(("parallel", "parallel", "arbitrary")),
        name=name,
    )(x, w3, stack)


def _softmax_tile_update(qt_ref, kb, vt, mask, m_ref, l_ref, acc_ref):
    for br in range(2):
        km = kb[:, br * D_HEAD:(br + 1) * D_HEAD]
        st = jnp.dot(km, qt_ref[br * D_HEAD:(br + 1) * D_HEAD, :], preferred_element_type=F32)
        if mask is not None:
            st = jnp.where(mask, st, NEG)
        m_prev = m_ref[br]
        m_new = jnp.maximum(m_prev, jnp.max(st, axis=0, keepdims=True))
        alpha = jnp.exp(m_prev - m_new)
        p = jnp.exp(st - m_new)
        l_ref[br] = alpha * l_ref[br] + jnp.sum(p, axis=0, keepdims=True)
        acc_ref[br] = alpha * acc_ref[br] + jnp.dot(vt, p.astype(BF16), preferred_element_type=F32)
        m_ref[br] = m_new


def _attn_init(q_ref, qt_ref, m_ref, l_ref, acc_ref):
    qt_ref[...] = q_ref[...].astype(F32).T.astype(BF16)
    m_ref[...] = jnp.full(m_ref.shape, NEG, F32)
    l_ref[...] = jnp.zeros(l_ref.shape, F32)
    acc_ref[...] = jnp.zeros(acc_ref.shape, F32)


def _attn_finish(lam_ref, subln_ref, o_ref, l_ref, acc_ref, lam_init):
    lp = lam_ref[...]
    lam = (jnp.exp(jnp.sum(lp[0:1] * lp[1:2], axis=-1, keepdims=True))
           - jnp.exp(jnp.sum(lp[2:3] * lp[3:4], axis=-1, keepdims=True)) + lam_init)
    ot = acc_ref[0] / l_ref[0] - lam * (acc_ref[1] / l_ref[1])
    ot = ot * lax.rsqrt(jnp.mean(ot * ot, axis=0, keepdims=True) + RMS_EPS)
    o_ref[...] = (ot.T * subln_ref[...] * (1.0 - lam_init)).astype(o_ref.dtype)


def _attn_prompt_kernel(lam_ref, subln_ref, q_ref, k_ref, v_ref, mix_in_ref, mix_ref, kb_ref, vt_ref, qt_ref,
                        m_ref, l_ref, acc_ref, *, tq, lam_init):
    del mix_in_ref
    i = pl.program_id(2)
    ntiles = kb_ref.shape[0]

    @pl.when(i == 0)
    def _():
        for j in range(ntiles):
            kb_ref[j] = k_ref[j * tq:(j + 1) * tq, :].astype(BF16)
            vt_ref[j] = v_ref[j * tq:(j + 1) * tq, :].T.astype(BF16)

    _attn_init(q_ref, qt_ref, m_ref, l_ref, acc_ref)

    def body(j, carry):
        _softmax_tile_update(qt_ref, kb_ref[j], vt_ref[j], None, m_ref, l_ref, acc_ref)
        return carry

    lax.fori_loop(0, i, body, 0)
    key_chunk = lax.broadcasted_iota(jnp.int32, (tq, tq), 0) // CHUNK
    qry_chunk = lax.broadcasted_iota(jnp.int32, (tq, tq), 1) // CHUNK
    _softmax_tile_update(qt_ref, kb_ref[i], vt_ref[i], key_chunk <= qry_chunk, m_ref, l_ref, acc_ref)
    _attn_finish(lam_ref, subln_ref, mix_ref, l_ref, acc_ref, lam_init)


def _attn_scratch(tq):
    return [pltpu.VMEM((HEAD_W, tq), BF16),
            pltpu.VMEM((2, 1, tq), F32),
            pltpu.VMEM((2, 1, tq), F32),
            pltpu.VMEM((2, HEAD_W, tq), F32)]


def _attn_prompt(q, k3, v3, layer, lam_p, subln, mix, *, batch, seq, lam_init):
    h_att = k3.shape[2] // HEAD_W
    tq = _pick(seq, 1024, CHUNK)
    nq = seq // tq
    kv_spec = pl.BlockSpec((None, seq, HEAD_W), lambda b, h, i: (layer, b, h))
    return pl.pallas_call(
        functools.partial(_attn_prompt_kernel, tq=tq, lam_init=lam_init),
        out_shape=jax.ShapeDtypeStruct(mix.shape, mix.dtype),
        grid=(batch, h_att, nq),
        in_specs=[pl.BlockSpec((4, D_HEAD), lambda b, h, i: (0, 0)),
                  pl.BlockSpec((1, HEAD_W), lambda b, h, i: (0, 0)),
                  pl.BlockSpec((tq, HEAD_W), lambda b, h, i: (b * nq + i, h)),
                  kv_spec, kv_spec, pl.BlockSpec(memory_space=pl.ANY)],
        out_specs=pl.BlockSpec((tq, HEAD_W), lambda b, h, i: (b * nq + i, h)),
        input_output_aliases={5: 0},
        scratch_shapes=[pltpu.VMEM((nq, tq, HEAD_W), BF16), pltpu.VMEM((nq, HEAD_W, tq), BF16)] + _attn_scratch(tq),
        compiler_params=_params(("parallel", "parallel", "arbitrary")),
        name="attn_prompt",
    )(lam_p, subln, q, k3, v3, mix)


def _softmax_rows_update(qh, kbs, vb, m_ref, l_ref, acc_ref, slot):
    for br in range(2):
        s = lax.dot_general(qh[:, br * D_HEAD:(br + 1) * D_HEAD], kbs[br],
                            (((1,), (1,)), ((), ())), preferred_element_type=F32)
        m_prev = m_ref[slot + br]
        m_new = jnp.maximum(m_prev, jnp.max(s, axis=-1, keepdims=True))
        alpha = jnp.exp(m_prev - m_new)
        p = jnp.exp(s - m_new)
        l_ref[slot + br] = alpha * l_ref[slot + br] + jnp.sum(p, axis=-1, keepdims=True)
        acc_ref[slot + br] = alpha * acc_ref[slot + br] + jnp.dot(p.astype(BF16), vb, preferred_element_type=F32)
        m_ref[slot + br] = m_new


def _attn_sample_kernel(lam_ref, subln_ref, q_ref, kn_ref, vn_ref, k1_ref, k2_ref, v1_ref, v2_ref, mix_in_ref,
                        mix_ref, m_ref, l_ref, acc_ref, *, tp, h_att, nchunks, lam_init):
    del mix_in_ref
    c = pl.program_id(1)

    @pl.when(c == 0)
    def _():
        m_ref[...] = jnp.full(m_ref.shape, NEG, F32)
        l_ref[...] = jnp.zeros(l_ref.shape, F32)
        acc_ref[...] = jnp.zeros(acc_ref.shape, F32)

    def head_rows(ref, h):
        return ref[pl.ds(h, tp, stride=h_att), :].astype(BF16)

    for h in range(h_att):
        vb = jnp.concatenate([head_rows(v1_ref, h), head_rows(v2_ref, h)], axis=1)
        _softmax_rows_update(q_ref[:, h * HEAD_W:(h + 1) * HEAD_W], (head_rows(k1_ref, h), head_rows(k2_ref, h)),
                             vb, m_ref, l_ref, acc_ref, 2 * h)

    @pl.when(c == nchunks - 1)
    def _():
        lp = lam_ref[...]
        lam = (jnp.exp(jnp.sum(lp[0:1] * lp[1:2], axis=-1, keepdims=True))
               - jnp.exp(jnp.sum(lp[2:3] * lp[3:4], axis=-1, keepdims=True)) + lam_init)
        for h in range(h_att):
            cols = slice(h * HEAD_W, (h + 1) * HEAD_W)
            kn = kn_ref[:, cols].astype(BF16)
            _softmax_rows_update(q_ref[:, cols], (kn[:, :D_HEAD], kn[:, D_HEAD:]), vn_ref[:, cols].astype(BF16),
                                 m_ref, l_ref, acc_ref, 2 * h)
            o = acc_ref[2 * h] / l_ref[2 * h] - lam * (acc_ref[2 * h + 1] / l_ref[2 * h + 1])
            o = o * lax.rsqrt(jnp.mean(o * o, axis=-1, keepdims=True) + RMS_EPS)
            mix_ref[:, cols] = (o * subln_ref[...] * (1.0 - lam_init)).astype(mix_ref.dtype)


def _attn_sample(q, k3, v3, cache_k, cache_v, layer, lam_p, subln, mix, *, row0, batch, seq, lam_init):
    h_att = cache_k.shape[3]
    d_att = h_att * HEAD_W
    past = cache_k.shape[2]
    assert past % CHUNK == 0 and seq <= CHUNK and row0 % seq == 0
    rb0 = row0 // seq
    tp = _pick(past, 1024, LANES)
    nchunks = past // tp
    new_spec = pl.BlockSpec((None, seq, d_att), lambda b, c: (layer, b, 0))
    cache_k = cache_k.reshape(cache_k.shape[0], batch, past * h_att, HEAD_W)
    cache_v = cache_v.reshape(cache_v.shape[0], batch, past * h_att, HEAD_W)
    past_lo = pl.BlockSpec((None, None, tp * h_att, D_HEAD), lambda b, c: (layer, b, c, 0))
    past_hi = pl.BlockSpec((None, None, tp * h_att, D_HEAD), lambda b, c: (layer, b, c, 1))
    return pl.pallas_call(
        functools.partial(_attn_sample_kernel, tp=tp, h_att=h_att, nchunks=nchunks, lam_init=lam_init),
        out_shape=jax.ShapeDtypeStruct(mix.shape, mix.dtype),
        grid=(batch, nchunks),
        in_specs=[pl.BlockSpec((4, D_HEAD), lambda b, c: (0, 0)),
                  pl.BlockSpec((1, HEAD_W), lambda b, c: (0, 0)),
                  pl.BlockSpec((seq, d_att), lambda b, c: (rb0 + b, 0)),
                  new_spec, new_spec, past_lo, past_hi, past_lo, past_hi,
                  pl.BlockSpec(memory_space=pl.ANY)],
        out_specs=pl.BlockSpec((seq, d_att), lambda b, c: (rb0 + b, 0)),
        input_output_aliases={9: 0},
        scratch_shapes=[pltpu.VMEM((2 * h_att, seq, 1), F32), pltpu.VMEM((2 * h_att, seq, 1), F32),
                        pltpu.VMEM((2 * h_att, seq, HEAD_W), F32)],
        compiler_params=_params(("parallel", "arbitrary")),
        name="attn_sample",
    )(lam_p, subln, q, k3, v3, cache_k, cache_k, cache_v, cache_v, mix)


def _dot_f32(a, b):
    return jnp.dot(a, b, precision=lax.Precision.HIGHEST, preferred_element_type=F32)


def _lane_group_sum(x):
    x = x + pltpu.roll(x, 2 * (LANES // 4), 1)
    return x + pltpu.roll(x, LANES // 4, 1)


def _head_sum(x):
    acc = x[:, 0:LANES]
    for j in range(1, x.shape[1] // LANES):
        acc = acc + x[:, j * LANES:(j + 1) * LANES]
    return _lane_group_sum(acc)


def _tile_lanes(slab, width):
    return jnp.concatenate([slab] * (width // LANES), axis=1)


def _rwkv_prep_kernel(ur_ref, uk_ref, uv_ref, ul_ref, halo_ref, mu_ref, w0_ref, wup_ref, a0_ref, aup_ref,
                      gup_ref, kkw_ref, kaw_ref, rkw_ref,
                      r_out, w_out, k_out, vs_out, kk_out, b_out, bonus_out, g_out, *, tt, d_rwkv):
    nseg = tt // SEG
    first_row = lax.broadcasted_iota(jnp.int32, (SEG, 1), 0) == 0

    def shifted(u_ref, col0, width):
        parts = []
        for s in range(nseg):
            u = u_ref[s * SEG:(s + 1) * SEG, :]
            prev = jnp.where(first_row, halo_ref[s:s + 1, col0:col0 + width], pltpu.roll(u, 1, 0))
            parts.append(u + (prev - u) * mu_ref[:, col0:col0 + width])
        return jnp.concatenate(parts, axis=0) if nseg > 1 else parts[0]

    r = shifted(ur_ref, 0, d_rwkv)
    k = shifted(uk_ref, d_rwkv, d_rwkv)
    v = shifted(uv_ref, 2 * d_rwkv, d_rwkv)
    lo = shifted(ul_ref, 3 * d_rwkv, LORA_PAD)
    wd = lo[:, LORA_W_OFF:LORA_A_OFF]
    ad = lo[:, LORA_A_OFF:LORA_G_OFF]
    gd = lo[:, LORA_G_OFF:LORA_PAD]

    w = w0_ref[...] + _dot_f32(jnp.tanh(wd), wup_ref[...])
    x = -w
    w = -(jnp.maximum(x, 0.0) + jnp.log1p(jnp.exp(-jnp.abs(x)))) - 0.5
    rate = jnp.exp(w)
    pos = lax.broadcasted_iota(jnp.int32, (tt, 1), 0) % SCAN_CHUNK
    cum = rate
    shift = 1
    while shift < SCAN_CHUNK:
        cum = cum + jnp.where(pos >= shift, pltpu.roll(cum, shift, 0), 0.0)
        shift *= 2
    decay_to = jnp.exp(-cum)
    decay_before = jnp.exp(rate - cum)
    undo_decay = jnp.exp(cum)
    a = jax.nn.sigmoid(a0_ref[...] + _dot_f32(ad, aup_ref[...]))
    g = _dot_f32(jax.nn.sigmoid(gd), gup_ref[...])

    kk = k * kkw_ref[...]
    inv = 1.0 / jnp.maximum(jnp.sqrt(_head_sum(kk * kk)), 1e-12)
    kk = kk * _tile_lanes(inv, d_rwkv)
    kmod = k * (1.0 + (a - 1.0) * kaw_ref[...])
    coef = _head_sum(r * kmod * rkw_ref[...])

    r_out[...] = r * decay_to
    w_out[...] = decay_to
    k_out[...] = kmod * undo_decay
    kk_out[...] = kk * decay_before
    b_out[...] = kk * a * undo_decay
    bonus_out[...] = _tile_lanes(coef, d_rwkv) * v
    g_out[...] = g

    group = lax.broadcasted_iota(jnp.int32, (tt, LANES), 1) // (LANES // 4)
    for j in range(d_rwkv // LANES):
        slab = v[:, j * LANES:(j + 1) * LANES]
        for gi in range(4):
            rep = _lane_group_sum(jnp.where(group == gi, slab, 0.0))
            vs_out[:, 4 * j + gi] = rep.reshape(tt // SUBLANES, SUBLANES, LANES)


def _rwkv_prep(rkv, lora, halo, mu, w0, wup, a0, aup, gup, kkw, kaw, rkw):
    n = rkv.shape[0]
    d_rwkv = rkv.shape[1] // 3
    tt = PREP_ROWS
    assert n % tt == 0 and d_rwkv // N_RWKV == LANES // 4
    row = lambda width: pl.BlockSpec((1, width), lambda i: (0, 0))
    full = lambda a: pl.BlockSpec(a.shape, lambda i: (0, 0))
    tile = pl.BlockSpec((tt, d_rwkv), lambda i: (i, 0))
    nat = jax.ShapeDtypeStruct((n, d_rwkv), F32)
    return pl.pallas_call(
        functools.partial(_rwkv_prep_kernel, tt=tt, d_rwkv=d_rwkv),
        out_shape=[nat, nat, nat, jax.ShapeDtypeStruct((n // SUBLANES, N_RWKV, SUBLANES, LANES), F32),
                   nat, nat, nat, nat],
        grid=(n // tt,),
        in_specs=[pl.BlockSpec((tt, d_rwkv), lambda i: (i, 0)),
                  pl.BlockSpec((tt, d_rwkv), lambda i: (i, 1)),
                  pl.BlockSpec((tt, d_rwkv), lambda i: (i, 2)),
                  pl.BlockSpec((tt, LORA_PAD), lambda i: (i, 0)),
                  pl.BlockSpec((None, tt // SEG, halo.shape[-1]), lambda i: (i, 0, 0)),
                  row(mu.shape[-1]), row(d_rwkv), full(wup), row(d_rwkv), full(aup), full(gup),
                  row(d_rwkv), row(d_rwkv), row(d_rwkv)],
        out_specs=[tile, tile, tile,
                   pl.BlockSpec((tt // SUBLANES, N_RWKV, SUBLANES, LANES), lambda i: (i, 0, 0, 0)),
                   tile, tile, tile, tile],
        compiler_params=_params(("parallel",)),
        name="rwkv_prep",
    )(rkv, rkv, rkv, lora, halo, mu, w0, wup, a0, aup, gup, kkw, kaw, rkw)


SCAN_SEQS = 2
SCAN_VROWS = 64
SCAN_PARTIALS = 4


def _rwkv_scan_kernel(*refs, tc, nchunks):
    nops = 5
    op_refs = [refs[e * nops:(e + 1) * nops] for e in range(SCAN_SEQS)]
    v_refs = refs[SCAN_SEQS * nops:SCAN_SEQS * (nops + 1)]
    s0_ref, o_ref, sout_ref, s_ref = refs[SCAN_SEQS * (nops + 1):]
    c = pl.program_id(1)
    nslab = s_ref.shape[1]

    @pl.when(c == 0)
    def _():
        s_ref[...] = s0_ref[...]

    vparts = N_RWKV // SCAN_VROWS

    def token(t8, i, e, p):
        base = pl.multiple_of(t8 * SUBLANES, SUBLANES)
        r_ref, _, k_ref, kk_ref, b_ref = (ref.at[pl.ds(base, SUBLANES), :] for ref in op_refs[e])
        vrows = slice(p * SCAN_VROWS, (p + 1) * SCAN_VROWS)

        def row(ref, j):
            return ref[i:i + 1, j * LANES:(j + 1) * LANES]

        vt = v_refs[e][t8, pl.ds(p * SCAN_VROWS * SUBLANES + i, SCAN_VROWS, stride=SUBLANES), :]
        sa = [None] * SCAN_PARTIALS
        for j in range(nslab):
            term = s_ref[e, j, vrows, :] * row(kk_ref, j)
            sa[j % SCAN_PARTIALS] = term if sa[j % SCAN_PARTIALS] is None else sa[j % SCAN_PARTIALS] + term
        sa = _lane_group_sum(functools.reduce(lambda a, c: a + c, sa))
        oo = [None] * SCAN_PARTIALS
        for j in range(nslab):
            s_new = s_ref[e, j, vrows, :] + (vt * row(k_ref, j) - sa * row(b_ref, j))
            s_ref[e, j, vrows, :] = s_new
            term = s_new * row(r_ref, j)
            oo[j % SCAN_PARTIALS] = term if oo[j % SCAN_PARTIALS] is None else oo[j % SCAN_PARTIALS] + term
        o_ref[e, base + i, vrows, :] = _lane_group_sum(functools.reduce(lambda a, c: a + c, oo))

    def step8(t8, carry):
        for i in range(SUBLANES):
            for e in range(SCAN_SEQS):
                for p in range(vparts):
                    token(t8, i, e, p)
        return carry

    lax.fori_loop(0, tc // SUBLANES, step8, 0)

    for e in range(SCAN_SEQS):
        decay_ref = op_refs[e][1]
        for j in range(nslab):
            s_ref[e, j] = s_ref[e, j] * decay_ref[tc - 1:tc, j * LANES:(j + 1) * LANES]

    @pl.when(c == nchunks - 1)
    def _():
        sout_ref[...] = s_ref[...]


def _rwkv_scan(r, w, k, kk, b, v_scan, s0, *, row0, batch, seq):
    d_rwkv = r.shape[1]
    nslab = d_rwkv // LANES
    tc = SCAN_CHUNK
    nchunks = seq // tc
    assert batch % SCAN_SEQS == 0 and row0 % tc == 0 and seq % tc == 0 and PREP_ROWS % tc == 0
    op_specs, v_specs = [], []
    for e in range(SCAN_SEQS):
        first = lambda g, c, e=e: (row0 + (SCAN_SEQS * g + e) * seq) // tc + c
        op_specs += [pl.BlockSpec((tc, d_rwkv), lambda g, c, f=first: (f(g, c), 0))] * 5
        v_specs.append(pl.BlockSpec((tc // SUBLANES, N_RWKV * SUBLANES, LANES), lambda g, c, f=first: (f(g, c), 0, 0)))
    state_spec = pl.BlockSpec((SCAN_SEQS, nslab, N_RWKV, LANES), lambda g, c: (g, 0, 0, 0))
    return pl.pallas_call(
        functools.partial(_rwkv_scan_kernel, tc=tc, nchunks=nchunks),
        out_shape=[jax.ShapeDtypeStruct((batch, seq, N_RWKV, LANES), F32),
                   jax.ShapeDtypeStruct((batch, nslab, N_RWKV, LANES), F32)],
        grid=(batch // SCAN_SEQS, nchunks),
        in_specs=op_specs + v_specs + [state_spec],
        out_specs=[pl.BlockSpec((SCAN_SEQS, tc, N_RWKV, LANES), lambda g, c: (g, c, 0, 0)), state_spec],
        scratch_shapes=[pltpu.VMEM((SCAN_SEQS, nslab, N_RWKV, LANES), F32)],
        compiler_params=_params(("parallel", "arbitrary")),
        name="rwkv_scan",
    )(*([r, w, k, kk, b] * SCAN_SEQS), *([v_scan] * SCAN_SEQS), s0)


def _rwkv_post_kernel(o_ref, bonus_ref, g_ref, lw_ref, lb_ref, mix_in_ref, out_ref, *, tt):
    del mix_in_ref
    d = bonus_ref.shape[1]
    group = lax.broadcasted_iota(jnp.int32, (tt, LANES), 1) // (LANES // 4)
    slabs = []
    for j in range(d // LANES):
        ch = [o_ref[pl.ds(4 * j + gi, tt, stride=N_RWKV), :] for gi in range(4)]
        slabs.append(jnp.where(group == 0, ch[0], jnp.where(group == 1, ch[1], jnp.where(group == 2, ch[2], ch[3]))))
    o = jnp.concatenate(slabs, axis=1)
    inv_n = 1.0 / N_RWKV
    dev = o - _tile_lanes(_head_sum(o) * inv_n, d)
    rstd = lax.rsqrt(_head_sum(dev * dev) * inv_n + LN_X_EPS)
    y = dev * _tile_lanes(rstd, d) * lw_ref[...] + lb_ref[...]
    out_ref[...] = ((y + bonus_ref[...]) * g_ref[...]).astype(out_ref.dtype)


def _rwkv_post(o_scan, bonus, g, lnx_w, lnx_b, mix, *, row0):
    rows = o_scan.shape[0] // N_RWKV
    d = bonus.shape[1]
    tt = _pick(math.gcd(rows, row0) or rows, 256, SUBLANES)
    rb = row0 // tt
    tile = pl.BlockSpec((tt, d), lambda i: (rb + i, 0))
    row = pl.BlockSpec((1, d), lambda i: (0, 0))
    return pl.pallas_call(
        functools.partial(_rwkv_post_kernel, tt=tt),
        out_shape=jax.ShapeDtypeStruct(mix.shape, mix.dtype),
        grid=(rows // tt,),
        in_specs=[pl.BlockSpec((tt * N_RWKV, LANES), lambda i: (i, 0)), tile, tile, row, row,
                  pl.BlockSpec(memory_space=pl.ANY)],
        out_specs=pl.BlockSpec((tt, d), lambda i: (rb + i, 1)),
        input_output_aliases={5: 0},
        compiler_params=_params(("parallel",)),
        name="rwkv_post",
    )(o_scan, bonus, g, lnx_w, lnx_b, mix)


def _channel_major(a, axis=-1):
    axis = axis % a.ndim
    h = a.shape[axis] // N_RWKV
    y = a.reshape(a.shape[:axis] + (h, N_RWKV) + a.shape[axis + 1:])
    return jnp.swapaxes(y, axis, axis + 1).reshape(a.shape)


def _head_major(a, axis=-1):
    axis = axis % a.ndim
    h = a.shape[axis] // N_RWKV
    y = a.reshape(a.shape[:axis] + (N_RWKV, h) + a.shape[axis + 1:])
    return jnp.swapaxes(y, axis, axis + 1).reshape(a.shape)


def _state_to_scan(s):
    b, h, nv, nk = s.shape
    return s.reshape(b, h, nv, nk // 4, 4).transpose(0, 3, 2, 4, 1).reshape(b, nk // 4, nv, 4 * h)


def _state_from_scan(s):
    b, nslab, nv, lanes = s.shape
    h = lanes // 4
    return s.reshape(b, nslab, nv, 4, h).transpose(0, 4, 2, 1, 3).reshape(b, h, nv, nslab * 4)


def _rkv_channel_major(a, d_rwkv):
    parts = [_channel_major(a[..., i * d_rwkv:(i + 1) * d_rwkv]) for i in range(3)]
    return jnp.concatenate(parts + [a[..., 3 * d_rwkv:]], axis=-1)


def _rkv_head_major(a, d_rwkv):
    parts = [_head_major(a[..., i * d_rwkv:(i + 1) * d_rwkv]) for i in range(3)]
    return jnp.concatenate(parts + [a[..., 3 * d_rwkv:]], axis=-1)


def _pad_rwkv_cols(a, d_rwkv):
    z = jnp.zeros(a.shape[:-1] + (LORA_A_OFF - DECAY_LORA,), a.dtype)
    o = 3 * d_rwkv
    return jnp.concatenate([a[..., :o + DECAY_LORA], z,
                            a[..., o + DECAY_LORA:o + DECAY_LORA + AAA_LORA], z,
                            a[..., o + DECAY_LORA + AAA_LORA:]], axis=-1)


def _unpad_rwkv_cols(a, d_rwkv):
    o = 3 * d_rwkv
    return jnp.concatenate([a[..., :o + DECAY_LORA],
                            a[..., o + LORA_A_OFF:o + LORA_A_OFF + AAA_LORA],
                            a[..., o + LORA_G_OFF:]], axis=-1)


def _pad_rows(w, rows):
    return jnp.concatenate([w, jnp.zeros((rows - w.shape[0], w.shape[1]), w.dtype)], axis=0)


def kernel(x_prompt, x_sample, cache_k, cache_v, state_shift, state_wkv, norm_mix, w_in, lam_q1, lam_k1, lam_q2, lam_k2, attn_subln, shift_mu, decay_w0, decay_up, iclr_a0, iclr_up, gate_up, key_kk, key_ka, bonus_rk, lnx_w, lnx_b, w_out, norm_ffn, ffn_up, ffn_down, norm_final):
    bp, tp, d_model = x_prompt.shape
    bs, ts, _ = x_sample.shape
    depth = w_in.shape[0]
    d_att = d_model // 2
    h_att = d_att // HEAD_W
    d_rwkv = d_model - d_att
    h_rwkv = d_rwkv // N_RWKV
    att_cols = 3 * d_att
    n_p, n_s = bp * tp, bs * ts
    n = n_p + n_s
    past = cache_k.shape[2]
    assert tp % SEG == 0 and ts % SEG == 0 and SEG == CHUNK

    x = jnp.concatenate([x_prompt.reshape(n_p, d_model), x_sample.reshape(n_s, d_model)], axis=0)
    rkv_cols = att_cols + 3 * d_rwkv
    w_lora = _pad_rwkv_cols(w_in[:, :, rkv_cols:], 0)
    w_rkv = _rkv_channel_major(w_in[:, :, att_cols:rkv_cols].astype(BF16), d_rwkv)
    w_out_cm = jnp.concatenate([w_out[:, :d_att].astype(BF16),
                                _channel_major(w_out[:, d_att:].astype(BF16), axis=1)], axis=1)
    row_cm = lambda a: _channel_major(a.reshape(-1)).reshape(1, -1)

    seg_p = jnp.arange(n_p // SEG) % (tp // SEG) == 0
    seg_s = jnp.arange(n_s // SEG) % (ts // SEG) == 0

    outs = {name: [] for name in ("sp", "wp", "ss", "ws")}
    kp = jnp.zeros((depth, n_p, d_att), F32)
    vp = jnp.zeros((depth, n_p, d_att), F32)
    ks = jnp.zeros((depth, n_s, d_att), F32)
    vs = jnp.zeros((depth, n_s, d_att), F32)
    mix = jnp.zeros((n, d_model), BF16)
    for l in range(depth):
        lam_init = 0.8 - 0.6 * math.exp(-0.3 * l)
        xn = _rmsnorm(x, norm_mix[l], BF16)
        q = _matmul(xn, w_in, l, col0=0, ncols=d_att, scale=D_HEAD ** -0.5, out_dtype=BF16, name="w_in_q")
        kp = _matmul_stacked(xn, w_in, l, kp, col0=d_att, row0=0, name="w_in_k")
        vp = _matmul_stacked(xn, w_in, l, vp, col0=2 * d_att, row0=0, name="w_in_v")
        ks = _matmul_stacked(xn, w_in, l, ks, col0=d_att, row0=n_p, name="w_in_ks")
        vs = _matmul_stacked(xn, w_in, l, vs, col0=2 * d_att, row0=n_p, name="w_in_vs")
        rkv = _matmul(xn, w_rkv, l, name="w_in_rkv")
        lora = _matmul(xn, w_lora, l, name="w_in_lora")

        lam_p = jnp.stack([lam_q1[l], lam_k1[l], lam_q2[l], lam_k2[l]])
        subln = attn_subln[l].reshape(1, HEAD_W)
        mix = _attn_prompt(q, kp, vp, l, lam_p, subln, mix, batch=bp, seq=tp, lam_init=lam_init)
        mix = _attn_sample(q, ks, vs, cache_k, cache_v, l, lam_p, subln, mix, row0=n_p, batch=bs, seq=ts,
                           lam_init=lam_init)

        u_last = jnp.concatenate([rkv[SEG - 1::SEG], lora[SEG - 1::SEG]], axis=1)
        prev = jnp.concatenate([jnp.zeros_like(u_last[:1]), u_last[:-1]], axis=0)
        st_p = jnp.zeros((n_p // SEG, u_last.shape[-1]), F32)
        shift_in = _rkv_channel_major(_pad_rwkv_cols(state_shift[l][:, 0, :], d_rwkv), d_rwkv)
        st_s = jnp.repeat(shift_in, ts // SEG, axis=0)
        start = jnp.concatenate([seg_p, seg_s])[:, None]
        halo = jnp.where(start, jnp.concatenate([st_p, st_s], axis=0), prev)
        halo = halo.reshape(n // PREP_ROWS, PREP_ROWS // SEG, halo.shape[-1])

        mu = _rkv_channel_major(_pad_rwkv_cols(shift_mu[l], d_rwkv), d_rwkv).reshape(1, -1)
        r, w, k, v_scan, kk, b, bonus, g = _rwkv_prep(
            rkv, lora, halo, mu,
            row_cm(decay_w0[l]), _pad_rows(_channel_major(decay_up[l]), LORA_A_OFF - LORA_W_OFF),
            row_cm(iclr_a0[l]), _pad_rows(_channel_major(iclr_up[l]), LORA_G_OFF - LORA_A_OFF),
            _channel_major(gate_up[l]), row_cm(key_kk[l]), row_cm(key_ka[l]), row_cm(bonus_rk[l]))
        v_scan = v_scan.reshape(n // SUBLANES, N_RWKV * SUBLANES, LANES)

        scan_ops = (r, w, k, kk, b, v_scan)
        s0_p = jnp.zeros((bp, d_rwkv // LANES, N_RWKV, LANES), F32)
        o_p, s_p = _rwkv_scan(*scan_ops, s0_p, row0=0, batch=bp, seq=tp)
        o_s, s_s = _rwkv_scan(*scan_ops, _state_to_scan(state_wkv[l]), row0=n_p, batch=bs, seq=ts)
        post_w = (row_cm(lnx_w[l]), row_cm(lnx_b[l]))
        mix = _rwkv_post(o_p.reshape(n_p * N_RWKV, LANES), bonus, g, *post_w, mix, row0=0)
        mix = _rwkv_post(o_s.reshape(n_s * N_RWKV, LANES), bonus, g, *post_w, mix, row0=n_p)

        x = _matmul(mix, w_out_cm, l, residual=x, name="w_out")
        hn = _rmsnorm(x, norm_ffn[l], BF16)
        hid = _matmul(hn, ffn_up, l, act="relu2", out_dtype=BF16, name="ffn_up")
        x = _matmul(hid, ffn_down, l, residual=x, name="ffn_down")

        u_fin = _rkv_head_major(u_last, d_rwkv).reshape(-1, 1, u_last.shape[-1])
        outs["sp"].append(_unpad_rwkv_cols(u_fin[tp // SEG - 1:n_p // SEG:tp // SEG], d_rwkv))
        outs["wp"].append(_state_from_scan(s_p))
        outs["ss"].append(_unpad_rwkv_cols(u_fin[n_p // SEG + ts // SEG - 1::ts // SEG], d_rwkv))
        outs["ws"].append(_state_from_scan(s_s))

    y_p = _rmsnorm(x, norm_final, F32, row0=0, rows=n_p)
    y_s = _rmsnorm(x, norm_final, F32, row0=n_p, rows=n_s)
    return (y_p.reshape(bp, tp, d_model), y_s.reshape(bs, ts, d_model),
            kp.reshape(depth, bp, tp, h_att, HEAD_W), vp.reshape(depth, bp, tp, h_att, HEAD_W),
            jnp.stack(outs["sp"]), jnp.stack(outs["wp"]),
            ks.reshape(depth, bs, ts, h_att, HEAD_W), vs.reshape(depth, bs, ts, h_att, HEAD_W),
            jnp.stack(outs["ss"]), jnp.stack(outs["ws"]))
```

```python
import functools
import math

import jax
import jax.numpy as jnp
from jax import lax
from jax.experimental import pallas as pl
from jax.experimental.pallas import tpu as pltpu

F32 = jnp.float32
BF16 = jnp.bfloat16

D_HEAD = 128
HEAD_W = 2 * D_HEAD
CHUNK = 64
N_RWKV = 64
DECAY_LORA = 96
AAA_LORA = 96
GATE_LORA = 256
RMS_EPS = 1e-5
LN_X_EPS = 64e-5
NEG = -1e30

LANES = 128
SUBLANES = 8
VMEM_LIMIT = 56 * 1024 * 1024

LORA_W_OFF, LORA_A_OFF, LORA_G_OFF, LORA_PAD = 0, 128, 256, 512
SEG = 64
PREP_ROWS = 128
SCAN_CHUNK = 64


def _pick(n, pref, mult):
    best = None
    for d in range(mult, min(n, pref) + 1, mult):
        if n % d == 0:
            best = d
    assert best is not None, (n, pref, mult)
    return best


def _params(sem, flags=None):
    return pltpu.CompilerParams(dimension_semantics=sem, vmem_limit_bytes=VMEM_LIMIT, flags=flags)


def _rmsnorm_kernel(x_ref, g_ref, o_ref):
    x = x_ref[...]
    ms = jnp.mean(x * x, axis=-1, keepdims=True)
    o_ref[...] = (x * lax.rsqrt(ms + RMS_EPS) * g_ref[...]).astype(o_ref.dtype)


def _rmsnorm(x, g, out_dtype, row0=0, rows=None):
    n, d = x.shape
    rows = n - row0 if rows is None else rows
    tr = _pick(math.gcd(rows, row0) or rows, 256, SUBLANES)
    rb = row0 // tr
    return pl.pallas_call(
        _rmsnorm_kernel,
        out_shape=jax.ShapeDtypeStruct((rows, d), out_dtype),
        grid=(rows // tr,),
        in_specs=[pl.BlockSpec((tr, d), lambda i: (rb + i, 0)),
                  pl.BlockSpec((1, d), lambda i: (0, 0))],
        out_specs=pl.BlockSpec((tr, d), lambda i: (i, 0)),
        compiler_params=_params(("parallel",)),
        name="rmsnorm",
    )(x, g.reshape(1, d))


def _mm_kernel(*refs, nk, has_res, has_alias, act, scale):
    refs = list(refs)
    x_ref, w_ref = refs[0], refs[1]
    r_ref = refs[2] if has_res else None
    o_ref = refs[2 + has_res + has_alias]
    acc_ref = refs[-1] if nk > 1 else None

    def epilogue(y):
        if act == "relu2":
            y = jnp.square(jnp.maximum(y, 0.0))
        if scale is not None:
            y = y * scale
        if has_res:
            y = y + r_ref[...]
        o_ref[...] = y.astype(o_ref.dtype).reshape(o_ref.shape)

    if nk == 1:
        epilogue(jnp.dot(x_ref[...], w_ref[...].astype(BF16), preferred_element_type=F32))
        return
    k = pl.program_id(2)

    @pl.when(k == 0)
    def _():
        acc_ref[...] = jnp.zeros_like(acc_ref)

    acc_ref[...] += jnp.dot(x_ref[...], w_ref[...].astype(BF16), preferred_element_type=F32)

    @pl.when(k == nk - 1)
    def _():
        epilogue(acc_ref[...])


FULL_K = 4096


def _mm_tiles(kd):
    return (1024, 512, kd) if kd <= FULL_K else (1024, 1024, 2048)


def _matmul(x, w3, layer, *, col0=0, ncols=None, residual=None, act=None, scale=None, out_dtype=F32,
            name="matmul"):
    m, kd = x.shape
    ncols = w3.shape[2] - col0 if ncols is None else ncols
    tm, tn, tk = _mm_tiles(kd)
    tm, tn, tk = _pick(m, tm, SUBLANES), _pick(math.gcd(ncols, col0) or ncols, tn, LANES), _pick(kd, tk, LANES)
    nk = kd // tk
    cb = col0 // tn
    in_specs = [pl.BlockSpec((tm, tk), lambda i, j, k: (i, k)),
                pl.BlockSpec((None, tk, tn), lambda i, j, k: (layer, k, cb + j))]
    args = [x, w3]
    if residual is not None:
        in_specs.append(pl.BlockSpec((tm, tn), lambda i, j, k: (i, j)))
        args.append(residual)
    return pl.pallas_call(
        functools.partial(_mm_kernel, nk=nk, has_res=residual is not None, has_alias=False, act=act, scale=scale),
        out_shape=jax.ShapeDtypeStruct((m, ncols), out_dtype),
        grid=(m // tm, ncols // tn, nk),
        in_specs=in_specs,
        out_specs=pl.BlockSpec((tm, tn), lambda i, j, k: (i, j)),
        scratch_shapes=[pltpu.VMEM((tm, tn), F32)] if nk > 1 else [],
        compiler_params=_params(("parallel", "parallel", "arbitrary")),
        name=name,
    )(*args)


def _matmul_stacked(x, w3, layer, stack, *, col0, row0, name="matmul_stacked"):
    m, kd = x.shape
    depth, rows, ncols = stack.shape
    tm, tn, tk = _mm_tiles(kd)
    tm, tn, tk = _pick(math.gcd(rows, row0) or rows, tm, SUBLANES), _pick(math.gcd(ncols, col0), tn, LANES), \
        _pick(kd, tk, LANES)
    nk = kd // tk
    rb, cb = row0 // tm, col0 // tn
    return pl.pallas_call(
        functools.partial(_mm_kernel, nk=nk, has_res=False, has_alias=True, act=None, scale=None),
        out_shape=jax.ShapeDtypeStruct(stack.shape, stack.dtype),
        grid=(rows // tm, ncols // tn, nk),
        in_specs=[pl.BlockSpec((tm, tk), lambda i, j, k: (rb + i, k)),
                  pl.BlockSpec((None, tk, tn), lambda i, j, k: (layer, k, cb + j)),
                  pl.BlockSpec(memory_space=pl.ANY)],
        out_specs=pl.BlockSpec((None, tm, tn), lambda i, j, k: (layer, i, j)),
        scratch_shapes=[pltpu.VMEM((tm, tn), F32)] if nk > 1 else [],
        input_output_aliases={2: 0},
        compiler_params=_params(("parallel", "parallel", "arbitrary")),
        name=name,
    )(x, w3, stack)


def _softmax_tile_update(qt_ref, kb, vt, mask, m_ref, l_ref, acc_ref):
    for br in range(2):
        km = kb[:, br * D_HEAD:(br + 1) * D_HEAD]
        st = jnp.dot(km, qt_ref[br * D_HEAD:(br + 1) * D_HEAD, :], preferred_element_type=F32)
        if mask is not None:
            st = jnp.where(mask, st, NEG)
        m_prev = m_ref[br]
        m_new = jnp.maximum(m_prev, jnp.max(st, axis=0, keepdims=True))
        alpha = jnp.exp(m_prev - m_new)
        p = jnp.exp(st - m_new)
        l_ref[br] = alpha * l_ref[br] + jnp.sum(p, axis=0, keepdims=True)
        acc_ref[br] = alpha * acc_ref[br] + jnp.dot(vt, p.astype(BF16), preferred_element_type=F32)
        m_ref[br] = m_new


def _attn_init(q_ref, qt_ref, m_ref, l_ref, acc_ref):
    qt_ref[...] = q_ref[...].astype(F32).T.astype(BF16)
    m_ref[...] = jnp.full(m_ref.shape, NEG, F32)
    l_ref[...] = jnp.zeros(l_ref.shape, F32)
    acc_ref[...] = jnp.zeros(acc_ref.shape, F32)


def _attn_finish(lam_ref, subln_ref, o_ref, l_ref, acc_ref, lam_init):
    lp = lam_ref[...]
    lam = (jnp.exp(jnp.sum(lp[0:1] * lp[1:2], axis=-1, keepdims=True))
           - jnp.exp(jnp.sum(lp[2:3] * lp[3:4], axis=-1, keepdims=True)) + lam_init)
    ot = acc_ref[0] / l_ref[0] - lam * (acc_ref[1] / l_ref[1])
    ot = ot * lax.rsqrt(jnp.mean(ot * ot, axis=0, keepdims=True) + RMS_EPS)
    o_ref[...] = (ot.T * subln_ref[...] * (1.0 - lam_init)).astype(o_ref.dtype)


def _attn_prompt_kernel(lam_ref, subln_ref, q_ref, k_ref, v_ref, mix_in_ref, mix_ref, kb_ref, vt_ref, qt_ref,
                        m_ref, l_ref, acc_ref, *, tq, lam_init):
    del mix_in_ref
    i = pl.program_id(2)
    ntiles = kb_ref.shape[0]

    @pl.when(i == 0)
    def _():
        for j in range(ntiles):
            kb_ref[j] = k_ref[j * tq:(j + 1) * tq, :].astype(BF16)
            vt_ref[j] = v_ref[j * tq:(j + 1) * tq, :].T.astype(BF16)

    _attn_init(q_ref, qt_ref, m_ref, l_ref, acc_ref)

    def body(j, carry):
        _softmax_tile_update(qt_ref, kb_ref[j], vt_ref[j], None, m_ref, l_ref, acc_ref)
        return carry

    lax.fori_loop(0, i, body, 0)
    key_chunk = lax.broadcasted_iota(jnp.int32, (tq, tq), 0) // CHUNK
    qry_chunk = lax.broadcasted_iota(jnp.int32, (tq, tq), 1) // CHUNK
    _softmax_tile_update(qt_ref, kb_ref[i], vt_ref[i], key_chunk <= qry_chunk, m_ref, l_ref, acc_ref)
    _attn_finish(lam_ref, subln_ref, mix_ref, l_ref, acc_ref, lam_init)


def _attn_scratch(tq):
    return [pltpu.VMEM((HEAD_W, tq), BF16),
            pltpu.VMEM((2, 1, tq), F32),
            pltpu.VMEM((2, 1, tq), F32),
            pltpu.VMEM((2, HEAD_W, tq), F32)]


def _attn_prompt(q, k3, v3, layer, lam_p, subln, mix, *, batch, seq, lam_init):
    h_att = k3.shape[2] // HEAD_W
    tq = _pick(seq, 1024, CHUNK)
    nq = seq // tq
    kv_spec = pl.BlockSpec((None, seq, HEAD_W), lambda b, h, i: (layer, b, h))
    return pl.pallas_call(
        functools.partial(_attn_prompt_kernel, tq=tq, lam_init=lam_init),
        out_shape=jax.ShapeDtypeStruct(mix.shape, mix.dtype),
        grid=(batch, h_att, nq),
        in_specs=[pl.BlockSpec((4, D_HEAD), lambda b, h, i: (0, 0)),
                  pl.BlockSpec((1, HEAD_W), lambda b, h, i: (0, 0)),
                  pl.BlockSpec((tq, HEAD_W), lambda b, h, i: (b * nq + i, h)),
                  kv_spec, kv_spec, pl.BlockSpec(memory_space=pl.ANY)],
        out_specs=pl.BlockSpec((tq, HEAD_W), lambda b, h, i: (b * nq + i, h)),
        input_output_aliases={5: 0},
        scratch_shapes=[pltpu.VMEM((nq, tq, HEAD_W), BF16), pltpu.VMEM((nq, HEAD_W, tq), BF16)] + _attn_scratch(tq),
        compiler_params=_params(("parallel", "parallel", "arbitrary")),
        name="attn_prompt",
    )(lam_p, subln, q, k3, v3, mix)


def _softmax_rows_update(qh, kbs, vb, m_ref, l_ref, acc_ref, slot):
    for br in range(2):
        s = lax.dot_general(qh[:, br * D_HEAD:(br + 1) * D_HEAD], kbs[br],
                            (((1,), (1,)), ((), ())), preferred_element_type=F32)
        m_prev = m_ref[slot + br]
        m_new = jnp.maximum(m_prev, jnp.max(s, axis=-1, keepdims=True))
        alpha = jnp.exp(m_prev - m_new)
        p = jnp.exp(s - m_new)
        l_ref[slot + br] = alpha * l_ref[slot + br] + jnp.sum(p, axis=-1, keepdims=True)
        acc_ref[slot + br] = alpha * acc_ref[slot + br] + jnp.dot(p.astype(BF16), vb, preferred_element_type=F32)
        m_ref[slot + br] = m_new


def _attn_sample_kernel(lam_ref, subln_ref, q_ref, kn_ref, vn_ref, k1_ref, k2_ref, v1_ref, v2_ref, mix_in_ref,
                        mix_ref, m_ref, l_ref, acc_ref, *, tp, h_att, nchunks, lam_init):
    del mix_in_ref
    c = pl.program_id(1)

    @pl.when(c == 0)
    def _():
        m_ref[...] = jnp.full(m_ref.shape, NEG, F32)
        l_ref[...] = jnp.zeros(l_ref.shape, F32)
        acc_ref[...] = jnp.zeros(acc_ref.shape, F32)

    def head_rows(ref, h):
        return ref[pl.ds(h, tp, stride=h_att), :].astype(BF16)

    for h in range(h_att):
        vb = jnp.concatenate([head_rows(v1_ref, h), head_rows(v2_ref, h)], axis=1)
        _softmax_rows_update(q_ref[:, h * HEAD_W:(h + 1) * HEAD_W], (head_rows(k1_ref, h), head_rows(k2_ref, h)),
                             vb, m_ref, l_ref, acc_ref, 2 * h)

    @pl.when(c == nchunks - 1)
    def _():
        lp = lam_ref[...]
        lam = (jnp.exp(jnp.sum(lp[0:1] * lp[1:2], axis=-1, keepdims=True))
               - jnp.exp(jnp.sum(lp[2:3] * lp[3:4], axis=-1, keepdims=True)) + lam_init)
        for h in range(h_att):
            cols = slice(h * HEAD_W, (h + 1) * HEAD_W)
            kn = kn_ref[:, cols].astype(BF16)
            _softmax_rows_update(q_ref[:, cols], (kn[:, :D_HEAD], kn[:, D_HEAD:]), vn_ref[:, cols].astype(BF16),
                                 m_ref, l_ref, acc_ref, 2 * h)
            o = acc_ref[2 * h] / l_ref[2 * h] - lam * (acc_ref[2 * h + 1] / l_ref[2 * h + 1])
            o = o * lax.rsqrt(jnp.mean(o * o, axis=-1, keepdims=True) + RMS_EPS)
            mix_ref[:, cols] = (o * subln_ref[...] * (1.0 - lam_init)).astype(mix_ref.dtype)


def _attn_sample(q, k3, v3, cache_k, cache_v, layer, lam_p, subln, mix, *, row0, batch, seq, lam_init):
    h_att = cache_k.shape[3]
    d_att = h_att * HEAD_W
    past = cache_k.shape[2]
    assert past % CHUNK == 0 and seq <= CHUNK and row0 % seq == 0
    rb0 = row0 // seq
    tp = _pick(past, 1024, LANES)
    nchunks = past // tp
    new_spec = pl.BlockSpec((None, seq, d_att), lambda b, c: (layer, b, 0))
    cache_k = cache_k.reshape(cache_k.shape[0], batch, past * h_att, HEAD_W)
    cache_v = cache_v.reshape(cache_v.shape[0], batch, past * h_att, HEAD_W)
    past_lo = pl.BlockSpec((None, None, tp * h_att, D_HEAD), lambda b, c: (layer, b, c, 0))
    past_hi = pl.BlockSpec((None, None, tp * h_att, D_HEAD), lambda b, c: (layer, b, c, 1))
    return pl.pallas_call(
        functools.partial(_attn_sample_kernel, tp=tp, h_att=h_att, nchunks=nchunks, lam_init=lam_init),
        out_shape=jax.ShapeDtypeStruct(mix.shape, mix.dtype),
        grid=(batch, nchunks),
        in_specs=[pl.BlockSpec((4, D_HEAD), lambda b, c: (0, 0)),
                  pl.BlockSpec((1, HEAD_W), lambda b, c: (0, 0)),
                  pl.BlockSpec((seq, d_att), lambda b, c: (rb0 + b, 0)),
                  new_spec, new_spec, past_lo, past_hi, past_lo, past_hi,
                  pl.BlockSpec(memory_space=pl.ANY)],
        out_specs=pl.BlockSpec((seq, d_att), lambda b, c: (rb0 + b, 0)),
        input_output_aliases={9: 0},
        scratch_shapes=[pltpu.VMEM((2 * h_att, seq, 1), F32), pltpu.VMEM((2 * h_att, seq, 1), F32),
                        pltpu.VMEM((2 * h_att, seq, HEAD_W), F32)],
        compiler_params=_params(("parallel", "arbitrary")),
        name="attn_sample",
    )(lam_p, subln, q, k3, v3, cache_k, cache_k, cache_v, cache_v, mix)


def _dot_f32(a, b):
    return jnp.dot(a, b, precision=lax.Precision.HIGHEST, preferred_element_type=F32)


def _lane_group_sum(x):
    x = x + pltpu.roll(x, 2 * (LANES // 4), 1)
    return x + pltpu.roll(x, LANES // 4, 1)


def _head_sum(x):
    acc = x[:, 0:LANES]
    for j in range(1, x.shape[1] // LANES):
        acc = acc + x[:, j * LANES:(j + 1) * LANES]
    return _lane_group_sum(acc)


def _tile_lanes(slab, width):
    return jnp.concatenate([slab] * (width // LANES), axis=1)


def _rwkv_prep_kernel(ur_ref, uk_ref, uv_ref, ul_ref, pr_ref, pk_ref, pv_ref, pl_ref, state_ref, start_ref,
                      mu_ref, w0_ref, wup_ref, a0_ref, aup_ref, gup_ref, kkw_ref, kaw_ref, rkw_ref,
                      r_out, w_out, k_out, vs_out, kk_out, b_out, bonus_out, g_out, *, tt, d_rwkv):
    nseg = tt // SEG
    first_row = lax.broadcasted_iota(jnp.int32, (SEG, 1), 0) == 0

    def shifted(u_ref, before_ref, col0, width):
        parts = []
        for s in range(nseg):
            u = u_ref[s * SEG:(s + 1) * SEG, :]
            before = before_ref[SUBLANES - 1:SUBLANES, :] if s == 0 else u_ref[s * SEG - 1:s * SEG, :]
            row0 = jnp.where(start_ref[s:s + 1, 0:1] > 0.5, state_ref[s:s + 1, col0:col0 + width], before)
            prev = jnp.where(first_row, row0, pltpu.roll(u, 1, 0))
            parts.append(u + (prev - u) * mu_ref[:, col0:col0 + width])
        return jnp.concatenate(parts, axis=0) if nseg > 1 else parts[0]

    r = shifted(ur_ref, pr_ref, 0, d_rwkv)
    k = shifted(uk_ref, pk_ref, d_rwkv, d_rwkv)
    v = shifted(uv_ref, pv_ref, 2 * d_rwkv, d_rwkv)
    lo = shifted(ul_ref, pl_ref, 3 * d_rwkv, LORA_PAD)
    wd = lo[:, LORA_W_OFF:LORA_A_OFF]
    ad = lo[:, LORA_A_OFF:LORA_G_OFF]
    gd = lo[:, LORA_G_OFF:LORA_PAD]

    w = w0_ref[...] + _dot_f32(jnp.tanh(wd), wup_ref[...])
    x = -w
    w = -(jnp.maximum(x, 0.0) + jnp.log1p(jnp.exp(-jnp.abs(x)))) - 0.5
    rate = jnp.exp(w)
    pos = lax.broadcasted_iota(jnp.int32, (tt, 1), 0) % SCAN_CHUNK
    cum = rate
    shift = 1
    while shift < SCAN_CHUNK:
        cum = cum + jnp.where(pos >= shift, pltpu.roll(cum, shift, 0), 0.0)
        shift *= 2
    decay_to = jnp.exp(-cum)
    decay_before = jnp.exp(rate - cum)
    undo_decay = jnp.exp(cum)
    a = jax.nn.sigmoid(a0_ref[...] + _dot_f32(ad, aup_ref[...]))
    g = _dot_f32(jax.nn.sigmoid(gd), gup_ref[...])

    kk = k * kkw_ref[...]
    inv = 1.0 / jnp.maximum(jnp.sqrt(_head_sum(kk * kk)), 1e-12)
    kk = kk * _tile_lanes(inv, d_rwkv)
    kmod = k * (1.0 + (a - 1.0) * kaw_ref[...])
    coef = _head_sum(r * kmod * rkw_ref[...])

    r_out[...] = r * decay_to
    w_out[...] = decay_to
    k_out[...] = kmod * undo_decay
    kk_out[...] = kk * decay_before
    b_out[...] = kk * a * undo_decay
    bonus_out[...] = _tile_lanes(coef, d_rwkv) * v
    g_out[...] = g

    group = lax.broadcasted_iota(jnp.int32, (tt, LANES), 1) // (LANES // 4)
    for j in range(d_rwkv // LANES):
        slab = v[:, j * LANES:(j + 1) * LANES]
        for gi in range(4):
            rep = _lane_group_sum(jnp.where(group == gi, slab, 0.0))
            vs_out[:, 4 * j + gi] = rep.reshape(tt // SUBLANES, SUBLANES, LANES)


def _rwkv_prep(rkv, lora, halo, starts, mu, w0, wup, a0, aup, gup, kkw, kaw, rkw):
    n = rkv.shape[0]
    d_rwkv = rkv.shape[1] // 3
    tt = PREP_ROWS
    assert n % tt == 0 and d_rwkv // N_RWKV == LANES // 4
    row = lambda width: pl.BlockSpec((1, width), lambda i: (0, 0))
    full = lambda a: pl.BlockSpec(a.shape, lambda i: (0, 0))
    tile = pl.BlockSpec((tt, d_rwkv), lambda i: (i, 0))
    nat = jax.ShapeDtypeStruct((n, d_rwkv), F32)
    before = lambda i: jnp.maximum(i * (tt // SUBLANES) - 1, 0)
    return pl.pallas_call(
        functools.partial(_rwkv_prep_kernel, tt=tt, d_rwkv=d_rwkv),
        out_shape=[nat, nat, nat, jax.ShapeDtypeStruct((n // SUBLANES, N_RWKV, SUBLANES, LANES), F32),
                   nat, nat, nat, nat],
        grid=(n // tt,),
        in_specs=[pl.BlockSpec((tt, d_rwkv), lambda i: (i, 0)),
                  pl.BlockSpec((tt, d_rwkv), lambda i: (i, 1)),
                  pl.BlockSpec((tt, d_rwkv), lambda i: (i, 2)),
                  pl.BlockSpec((tt, LORA_PAD), lambda i: (i, 0)),
                  pl.BlockSpec((SUBLANES, d_rwkv), lambda i: (before(i), 0)),
                  pl.BlockSpec((SUBLANES, d_rwkv), lambda i: (before(i), 1)),
                  pl.BlockSpec((SUBLANES, d_rwkv), lambda i: (before(i), 2)),
                  pl.BlockSpec((SUBLANES, LORA_PAD), lambda i: (before(i), 0)),
                  pl.BlockSpec((None, tt // SEG, halo.shape[-1]), lambda i: (i, 0, 0)),
                  pl.BlockSpec((None, tt // SEG, LANES), lambda i: (i, 0, 0)),
                  row(mu.shape[-1]), row(d_rwkv), full(wup), row(d_rwkv), full(aup), full(gup),
                  row(d_rwkv), row(d_rwkv), row(d_rwkv)],
        out_specs=[tile, tile, tile,
                   pl.BlockSpec((tt // SUBLANES, N_RWKV, SUBLANES, LANES), lambda i: (i, 0, 0, 0)),
                   tile, tile, tile, tile],
        compiler_params=_params(("parallel",)),
        name="rwkv_prep",
    )(rkv, rkv, rkv, lora, rkv, rkv, rkv, lora, halo, starts, mu, w0, wup, a0, aup, gup, kkw, kaw, rkw)


SCAN_SEQS = 2
SCAN_VROWS = 64
SCAN_PARTIALS = 4


def _rwkv_scan_kernel(*refs, tc, nchunks):
    nops = 5
    op_refs = [refs[e * nops:(e + 1) * nops] for e in range(SCAN_SEQS)]
    v_refs = refs[SCAN_SEQS * nops:SCAN_SEQS * (nops + 1)]
    s0_ref, o_ref, sout_ref, s_ref = refs[SCAN_SEQS * (nops + 1):]
    c = pl.program_id(1)
    nslab = s_ref.shape[1]

    @pl.when(c == 0)
    def _():
        s_ref[...] = s0_ref[...]

    vparts = N_RWKV // SCAN_VROWS

    def token(t8, i, e, p):
        base = pl.multiple_of(t8 * SUBLANES, SUBLANES)
        r_ref, _, k_ref, kk_ref, b_ref = (ref.at[pl.ds(base, SUBLANES), :] for ref in op_refs[e])
        vrows = slice(p * SCAN_VROWS, (p + 1) * SCAN_VROWS)

        def row(ref, j):
            return ref[i:i + 1, j * LANES:(j + 1) * LANES]

        vt = v_refs[e][t8, pl.ds(p * SCAN_VROWS * SUBLANES + i, SCAN_VROWS, stride=SUBLANES), :]
        sa = [None] * SCAN_PARTIALS
        for j in range(nslab):
            term = s_ref[e, j, vrows, :] * row(kk_ref, j)
            sa[j % SCAN_PARTIALS] = term if sa[j % SCAN_PARTIALS] is None else sa[j % SCAN_PARTIALS] + term
        sa = _lane_group_sum(functools.reduce(lambda a, c: a + c, sa))
        oo = [None] * SCAN_PARTIALS
        for j in range(nslab):
            s_new = s_ref[e, j, vrows, :] + (vt * row(k_ref, j) - sa * row(b_ref, j))
            s_ref[e, j, vrows, :] = s_new
            term = s_new * row(r_ref, j)
            oo[j % SCAN_PARTIALS] = term if oo[j % SCAN_PARTIALS] is None else oo[j % SCAN_PARTIALS] + term
        o_ref[e, base + i, vrows, :] = _lane_group_sum(functools.reduce(lambda a, c: a + c, oo))

    def step8(t8, carry):
        for i in range(SUBLANES):
            for e in range(SCAN_SEQS):
                for p in range(vparts):
                    token(t8, i, e, p)
        return carry

    lax.fori_loop(0, tc // SUBLANES, step8, 0)

    for e in range(SCAN_SEQS):
        decay_ref = op_refs[e][1]
        for j in range(nslab):
            s_ref[e, j] = s_ref[e, j] * decay_ref[tc - 1:tc, j * LANES:(j + 1) * LANES]

    @pl.when(c == nchunks - 1)
    def _():
        sout_ref[...] = s_ref[...]


def _rwkv_scan(r, w, k, kk, b, v_scan, s0, *, row0, batch, seq):
    d_rwkv = r.shape[1]
    nslab = d_rwkv // LANES
    tc = SCAN_CHUNK
    nchunks = seq // tc
    assert batch % SCAN_SEQS == 0 and row0 % tc == 0 and seq % tc == 0 and PREP_ROWS % tc == 0
    op_specs, v_specs = [], []
    for e in range(SCAN_SEQS):
        first = lambda g, c, e=e: (row0 + (SCAN_SEQS * g + e) * seq) // tc + c
        op_specs += [pl.BlockSpec((tc, d_rwkv), lambda g, c, f=first: (f(g, c), 0))] * 5
        v_specs.append(pl.BlockSpec((tc // SUBLANES, N_RWKV * SUBLANES, LANES), lambda g, c, f=first: (f(g, c), 0, 0)))
    state_spec = pl.BlockSpec((SCAN_SEQS, nslab, N_RWKV, LANES), lambda g, c: (g, 0, 0, 0))
    return pl.pallas_call(
        functools.partial(_rwkv_scan_kernel, tc=tc, nchunks=nchunks),
        out_shape=[jax.ShapeDtypeStruct((batch, seq, N_RWKV, LANES), F32),
                   jax.ShapeDtypeStruct((batch, nslab, N_RWKV, LANES), F32)],
        grid=(batch // SCAN_SEQS, nchunks),
        in_specs=op_specs + v_specs + [state_spec],
        out_specs=[pl.BlockSpec((SCAN_SEQS, tc, N_RWKV, LANES), lambda g, c: (g, c, 0, 0)), state_spec],
        scratch_shapes=[pltpu.VMEM((SCAN_SEQS, nslab, N_RWKV, LANES), F32)],
        compiler_params=_params(("parallel", "arbitrary")),
        name="rwkv_scan",
    )(*([r, w, k, kk, b] * SCAN_SEQS), *([v_scan] * SCAN_SEQS), s0)


def _rwkv_post_kernel(o_ref, bonus_ref, g_ref, lw_ref, lb_ref, mix_in_ref, out_ref, *, tt):
    del mix_in_ref
    d = bonus_ref.shape[1]
    group = lax.broadcasted_iota(jnp.int32, (tt, LANES), 1) // (LANES // 4)
    slabs = []
    for j in range(d // LANES):
        ch = [o_ref[pl.ds(4 * j + gi, tt, stride=N_RWKV), :] for gi in range(4)]
        slabs.append(jnp.where(group == 0, ch[0], jnp.where(group == 1, ch[1], jnp.where(group == 2, ch[2], ch[3]))))
    o = jnp.concatenate(slabs, axis=1)
    inv_n = 1.0 / N_RWKV
    dev = o - _tile_lanes(_head_sum(o) * inv_n, d)
    rstd = lax.rsqrt(_head_sum(dev * dev) * inv_n + LN_X_EPS)
    y = dev * _tile_lanes(rstd, d) * lw_ref[...] + lb_ref[...]
    out_ref[...] = ((y + bonus_ref[...]) * g_ref[...]).astype(out_ref.dtype)


def _rwkv_post(o_scan, bonus, g, lnx_w, lnx_b, mix, *, row0):
    rows = o_scan.shape[0] // N_RWKV
    d = bonus.shape[1]
    tt = _pick(math.gcd(rows, row0) or rows, 256, SUBLANES)
    rb = row0 // tt
    tile = pl.BlockSpec((tt, d), lambda i: (rb + i, 0))
    row = pl.BlockSpec((1, d), lambda i: (0, 0))
    return pl.pallas_call(
        functools.partial(_rwkv_post_kernel, tt=tt),
        out_shape=jax.ShapeDtypeStruct(mix.shape, mix.dtype),
        grid=(rows // tt,),
        in_specs=[pl.BlockSpec((tt * N_RWKV, LANES), lambda i: (i, 0)), tile, tile, row, row,
                  pl.BlockSpec(memory_space=pl.ANY)],
        out_specs=pl.BlockSpec((tt, d), lambda i: (rb + i, 1)),
        input_output_aliases={5: 0},
        compiler_params=_params(("parallel",)),
        name="rwkv_post",
    )(o_scan, bonus, g, lnx_w, lnx_b, mix)


def _channel_major(a, axis=-1):
    axis = axis % a.ndim
    h = a.shape[axis] // N_RWKV
    y = a.reshape(a.shape[:axis] + (h, N_RWKV) + a.shape[axis + 1:])
    return jnp.swapaxes(y, axis, axis + 1).reshape(a.shape)


def _head_major(a, axis=-1):
    axis = axis % a.ndim
    h = a.shape[axis] // N_RWKV
    y = a.reshape(a.shape[:axis] + (N_RWKV, h) + a.shape[axis + 1:])
    return jnp.swapaxes(y, axis, axis + 1).reshape(a.shape)


def _state_to_scan(s):
    b, h, nv, nk = s.shape
    return s.reshape(b, h, nv, nk // 4, 4).transpose(0, 3, 2, 4, 1).reshape(b, nk // 4, nv, 4 * h)


def _state_from_scan(s):
    b, nslab, nv, lanes = s.shape
    h = lanes // 4
    return s.reshape(b, nslab, nv, 4, h).transpose(0, 4, 2, 1, 3).reshape(b, h, nv, nslab * 4)


def _rkv_channel_major(a, d_rwkv):
    parts = [_channel_major(a[..., i * d_rwkv:(i + 1) * d_rwkv]) for i in range(3)]
    return jnp.concatenate(parts + [a[..., 3 * d_rwkv:]], axis=-1)


def _rkv_head_major(a, d_rwkv):
    parts = [_head_major(a[..., i * d_rwkv:(i + 1) * d_rwkv]) for i in range(3)]
    return jnp.concatenate(parts + [a[..., 3 * d_rwkv:]], axis=-1)


def _pad_rwkv_cols(a, d_rwkv):
    z = jnp.zeros(a.shape[:-1] + (LORA_A_OFF - DECAY_LORA,), a.dtype)
    o = 3 * d_rwkv
    return jnp.concatenate([a[..., :o + DECAY_LORA], z,
                            a[..., o + DECAY_LORA:o + DECAY_LORA + AAA_LORA], z,
                            a[..., o + DECAY_LORA + AAA_LORA:]], axis=-1)


def _unpad_rwkv_cols(a, d_rwkv):
    o = 3 * d_rwkv
    return jnp.concatenate([a[..., :o + DECAY_LORA],
                            a[..., o + LORA_A_OFF:o + LORA_A_OFF + AAA_LORA],
                            a[..., o + LORA_G_OFF:]], axis=-1)


def _pad_rows(w, rows):
    return jnp.concatenate([w, jnp.zeros((rows - w.shape[0], w.shape[1]), w.dtype)], axis=0)


def kernel(x_prompt, x_sample, cache_k, cache_v, state_shift, state_wkv, norm_mix, w_in, lam_q1, lam_k1, lam_q2, lam_k2, attn_subln, shift_mu, decay_w0, decay_up, iclr_a0, iclr_up, gate_up, key_kk, key_ka, bonus_rk, lnx_w, lnx_b, w_out, norm_ffn, ffn_up, ffn_down, norm_final):
    bp, tp, d_model = x_prompt.shape
    bs, ts, _ = x_sample.shape
    depth = w_in.shape[0]
    d_att = d_model // 2
    h_att = d_att // HEAD_W
    d_rwkv = d_model - d_att
    h_rwkv = d_rwkv // N_RWKV
    att_cols = 3 * d_att
    n_p, n_s = bp * tp, bs * ts
    n = n_p + n_s
    past = cache_k.shape[2]
    assert tp % SEG == 0 and ts % SEG == 0 and SEG == CHUNK

    x = jnp.concatenate([x_prompt.reshape(n_p, d_model), x_sample.reshape(n_s, d_model)], axis=0)
    rkv_cols = att_cols + 3 * d_rwkv
    w_lora = _pad_rwkv_cols(w_in[:, :, rkv_cols:], 0)
    w_rkv = _rkv_channel_major(w_in[:, :, att_cols:rkv_cols].astype(BF16), d_rwkv)
    w_out_cm = jnp.concatenate([w_out[:, :d_att].astype(BF16),
                                _channel_major(w_out[:, d_att:].astype(BF16), axis=1)], axis=1)
    row_cm = lambda a: _channel_major(a.reshape(-1)).reshape(1, -1)

    seg_p = jnp.arange(n_p // SEG) % (tp // SEG) == 0
    seg_s = jnp.arange(n_s // SEG) % (ts // SEG) == 0
    starts = jnp.broadcast_to(jnp.concatenate([seg_p, seg_s]).astype(F32)[:, None], (n // SEG, LANES))
    starts = starts.reshape(n // PREP_ROWS, PREP_ROWS // SEG, LANES)

    outs = {name: [] for name in ("sp", "wp", "ss", "ws")}
    kp = jnp.zeros((depth, n_p, d_att), F32)
    vp = jnp.zeros((depth, n_p, d_att), F32)
    ks = jnp.zeros((depth, n_s, d_att), F32)
    vs = jnp.zeros((depth, n_s, d_att), F32)
    mix = jnp.zeros((n, d_model), BF16)
    for l in range(depth):
        lam_init = 0.8 - 0.6 * math.exp(-0.3 * l)
        xn = _rmsnorm(x, norm_mix[l], BF16)
        q = _matmul(xn, w_in, l, col0=0, ncols=d_att, scale=D_HEAD ** -0.5, out_dtype=BF16, name="w_in_q")
        kp = _matmul_stacked(xn, w_in, l, kp, col0=d_att, row0=0, name="w_in_k")
        vp = _matmul_stacked(xn, w_in, l, vp, col0=2 * d_att, row0=0, name="w_in_v")
        ks = _matmul_stacked(xn, w_in, l, ks, col0=d_att, row0=n_p, name="w_in_ks")
        vs = _matmul_stacked(xn, w_in, l, vs, col0=2 * d_att, row0=n_p, name="w_in_vs")
        rkv = _matmul(xn, w_rkv, l, name="w_in_rkv")
        lora = _matmul(xn, w_lora, l, name="w_in_lora")

        lam_p = jnp.stack([lam_q1[l], lam_k1[l], lam_q2[l], lam_k2[l]])
        subln = attn_subln[l].reshape(1, HEAD_W)
        mix = _attn_prompt(q, kp, vp, l, lam_p, subln, mix, batch=bp, seq=tp, lam_init=lam_init)
        mix = _attn_sample(q, ks, vs, cache_k, cache_v, l, lam_p, subln, mix, row0=n_p, batch=bs, seq=ts,
                           lam_init=lam_init)

        shift_in = _rkv_channel_major(_pad_rwkv_cols(state_shift[l][:, 0, :], d_rwkv), d_rwkv)
        st_p = jnp.zeros((n_p // SEG, shift_in.shape[-1]), F32)
        st_s = jnp.repeat(shift_in, ts // SEG, axis=0)
        halo = jnp.concatenate([st_p, st_s], axis=0).reshape(n // PREP_ROWS, PREP_ROWS // SEG, -1)

        mu = _rkv_channel_major(_pad_rwkv_cols(shift_mu[l], d_rwkv), d_rwkv).reshape(1, -1)
        r, w, k, v_scan, kk, b, bonus, g = _rwkv_prep(
            rkv, lora, halo, starts, mu,
            row_cm(decay_w0[l]), _pad_rows(_channel_major(decay_up[l]), LORA_A_OFF - LORA_W_OFF),
            row_cm(iclr_a0[l]), _pad_rows(_channel_major(iclr_up[l]), LORA_G_OFF - LORA_A_OFF),
            _channel_major(gate_up[l]), row_cm(key_kk[l]), row_cm(key_ka[l]), row_cm(bonus_rk[l]))
        v_scan = v_scan.reshape(n // SUBLANES, N_RWKV * SUBLANES, LANES)

        scan_ops = (r, w, k, kk, b, v_scan)
        s0_p = jnp.zeros((bp, d_rwkv // LANES, N_RWKV, LANES), F32)
        o_p, s_p = _rwkv_scan(*scan_ops, s0_p, row0=0, batch=bp, seq=tp)
        o_s, s_s = _rwkv_scan(*scan_ops, _state_to_scan(state_wkv[l]), row0=n_p, batch=bs, seq=ts)
        post_w = (row_cm(lnx_w[l]), row_cm(lnx_b[l]))
        mix = _rwkv_post(o_p.reshape(n_p * N_RWKV, LANES), bonus, g, *post_w, mix, row0=0)
        mix = _rwkv_post(o_s.reshape(n_s * N_RWKV, LANES), bonus, g, *post_w, mix, row0=n_p)

        x = _matmul(mix, w_out_cm, l, residual=x, name="w_out")
        hn = _rmsnorm(x, norm_ffn[l], BF16)
        hid = _matmul(hn, ffn_up, l, act="relu2", out_dtype=BF16, name="ffn_up")
        x = _matmul(hid, ffn_down, l, residual=x, name="ffn_down")

        def last_rows(lo, batch, t):
            rows = [a[lo:lo + batch * t].reshape(batch, t, -1)[:, -1:] for a in (rkv, lora)]
            return _unpad_rwkv_cols(_rkv_head_major(jnp.concatenate(rows, axis=-1), d_rwkv), d_rwkv)

        outs["sp"].append(last_rows(0, bp, tp))
        outs["wp"].append(_state_from_scan(s_p))
        outs["ss"].append(last_rows(n_p, bs, ts))
        outs["ws"].append(_state_from_scan(s_s))

    y_p = _rmsnorm(x, norm_final, F32, row0=0, rows=n_p)
    y_s = _rmsnorm(x, norm_final, F32, row0=n_p, rows=n_s)
    return (y_p.reshape(bp, tp, d_model), y_s.reshape(bs, ts, d_model),
            kp.reshape(depth, bp, tp, h_att, HEAD_W), vp.reshape(depth, bp, tp, h_att, HEAD_W),
            jnp.stack(outs["sp"]), jnp.stack(outs["wp"]),
            ks.reshape(depth, bs, ts, h_att, HEAD_W), vs.reshape(depth, bs, ts, h_att, HEAD_W),
            jnp.stack(outs["ss"]), jnp.stack(outs["ws"]))
```
